```python
import math
import jax, jax.numpy as jnp
from jax import lax
import numpy as np

D_MODEL = 1024
BATCH = 8
SEQ = 4096
DEPTH = 2

GRID_W = 64
CTX_LEN = 256
N_BRANCH = 4
BRANCH_W = D_MODEL // 4
HEAD_DIM = 64
NA_HEADS = BRANCH_W // HEAD_DIM
WIN_H = 8
WIN_W = 16
MLA_HEADS = 4
MLA_NOPE = 64
MLA_ROPE = 32
MLA_V = BRANCH_W // MLA_HEADS
MLA_Q_LORA = 256
MLA_KV_LORA = 128
LRU_W = BRANCH_W
LRU_BLOCKS = 4
LRU_BW = LRU_W // LRU_BLOCKS
CONV_W = 4
LRU_C = 8.0
GQA_HEADS = BRANCH_W // HEAD_DIM
GQA_KV_HEADS = 2
ROPE_THETA = 10000.0
Q_BLOCK = 128
EPS = 1e-6
NA_SCALE = HEAD_DIM ** -0.5
MLA_SCALE = (MLA_NOPE + MLA_ROPE) ** -0.5
GQA_SCALE = HEAD_DIM ** -0.5
DEEPNORM_ALPHA = (2 * DEPTH) ** 0.25
DEEPNORM_BETA = (8 * DEPTH) ** -0.25

MIX_SPLITS = (BRANCH_W, BRANCH_W, BRANCH_W,
              MLA_Q_LORA, MLA_KV_LORA, MLA_ROPE,
              LRU_W,
              GQA_HEADS * HEAD_DIM, GQA_KV_HEADS * HEAD_DIM, GQA_KV_HEADS * HEAD_DIM)
MIX_COLS = sum(MIX_SPLITS)
SILU_COLS = N_BRANCH * BRANCH_W
MERGE_COLS = N_BRANCH * D_MODEL
N_IN = MIX_COLS + SILU_COLS + MERGE_COLS

kernel_name = "hybrid_na_mla_rglru_gqa_prefix_block"


def layer_norm(x, eps=EPS):
    xf = x.astype(jnp.float32)
    mu = jnp.mean(xf, -1, keepdims=True)
    var = jnp.mean(jnp.square(xf - mu), -1, keepdims=True)
    return ((xf - mu) * lax.rsqrt(var + eps)).astype(x.dtype)


def rms_norm(x, g, eps=EPS):
    xf = x.astype(jnp.float32)
    y = xf * lax.rsqrt(jnp.mean(jnp.square(xf), -1, keepdims=True) + eps)
    return (y * g.astype(jnp.float32)).astype(x.dtype)


def rope_1d(x, pos):
    d = x.shape[-1]
    inv = ROPE_THETA ** (-jnp.arange(0, d, 2, dtype=jnp.float32) / d)
    ang = pos[:, None] * inv[None, :]
    cos, sin = jnp.cos(ang), jnp.sin(ang)
    xf = x.astype(jnp.float32)
    x1, x2 = xf[..., : d // 2], xf[..., d // 2:]
    return jnp.concatenate([x1 * cos - x2 * sin, x2 * cos + x1 * sin], -1).astype(x.dtype)


def axial_rope(x, rows, cols):
    half = x.shape[-1] // 2
    return jnp.concatenate([rope_1d(x[..., :half], rows), rope_1d(x[..., half:], cols)], -1)


def split_cols(p, sizes):
    out, start = [], 0
    for size in sizes:
        out.append(p[..., start:start + size])
        start += size
    return out


def split_heads(t, n_heads):
    b, n, _ = t.shape
    return t.reshape(b, n, n_heads, -1).transpose(0, 2, 1, 3)


def merge_heads(t):
    b, h, n, d = t.shape
    return t.transpose(0, 2, 1, 3).reshape(b, n, h * d)


def attend(q, keys, vals, scale):
    b, hq, nq, dk = q.shape
    hk = keys.shape[1]
    qg = q.reshape(b, hk, hq // hk, nq, dk)
    s = jnp.einsum('bkgqd,bknd->bkgqn', qg, keys, preferred_element_type=jnp.float32) * scale
    p = jax.nn.softmax(s, axis=-1).astype(vals.dtype)
    o = jnp.einsum('bkgqn,bknd->bkgqd', p, vals)
    return o.reshape(b, hq, nq, vals.shape[-1])


def prefix_attention(q, k, v, k_ctx, v_ctx, scale):
    b, hq, n, dk = q.shape
    keys = jnp.concatenate([k_ctx, k], axis=2)
    vals = jnp.concatenate([v_ctx, v], axis=2)
    nb = n // Q_BLOCK
    qb = q.reshape(b, hq, nb, Q_BLOCK, dk).transpose(2, 0, 1, 3, 4)
    out = lax.map(lambda qi: attend(qi, keys, vals, scale), qb)
    return out.transpose(1, 2, 0, 3, 4).reshape(b, hq, n, vals.shape[-1])


def neighbourhood_attention(q, k, v, k_ctx, v_ctx, rel_bias):
    b, h, n, hd = q.shape
    n_rows = n // GRID_W
    wh, ww = min(WIN_H, n_rows), WIN_W
    qg = q.reshape(b, h, n_rows, GRID_W, hd)
    kg = k.reshape(b, h, n_rows, GRID_W, hd)
    vg = v.reshape(b, h, n_rows, GRID_W, hd)
    col_start = np.clip(np.arange(GRID_W) - ww // 2, 0, GRID_W - ww)
    col_idx = col_start[:, None] + np.arange(ww)[None, :]
    dc = col_idx - np.arange(GRID_W)[:, None]
    row_start = np.clip(np.arange(n_rows) - wh // 2, 0, n_rows - wh)
    dr = row_start[:, None] + np.arange(wh)[None, :] - np.arange(n_rows)[:, None]
    bias = rel_bias[:, (dr + WIN_H - 1)[:, :, None, None], (dc + WIN_W - 1)[None, None, :, :]]
    bias = bias.transpose(1, 0, 3, 2, 4).reshape(n_rows, h, GRID_W, wh * ww)

    def row_fn(args):
        qr, rs, bias_r = args
        k_win = jnp.take(lax.dynamic_slice_in_dim(kg, rs, wh, axis=2), col_idx, axis=3)
        v_win = jnp.take(lax.dynamic_slice_in_dim(vg, rs, wh, axis=2), col_idx, axis=3)
        s_win = jnp.einsum('bhqd,bhiqjd->bhqij', qr, k_win, preferred_element_type=jnp.float32)
        s_win = s_win.reshape(b, h, GRID_W, wh * ww) * NA_SCALE + bias_r.astype(jnp.float32)
        s_ctx = jnp.einsum('bhqd,bhnd->bhqn', qr, k_ctx, preferred_element_type=jnp.float32) * NA_SCALE
        p = jax.nn.softmax(jnp.concatenate([s_win, s_ctx], -1), axis=-1).astype(v.dtype)
        p_win = p[..., :wh * ww].reshape(b, h, GRID_W, wh, ww)
        return (jnp.einsum('bhqij,bhiqjd->bhqd', p_win, v_win)
                + jnp.einsum('bhqn,bhnd->bhqd', p[..., wh * ww:], v_ctx))

    out = lax.map(row_fn, (qg.transpose(2, 0, 1, 3, 4), jnp.asarray(row_start, jnp.int32), bias))
    return out.transpose(1, 2, 0, 3, 4).reshape(b, h, n, hd)


def mla_project(cq, ckv, krope, q_norm, w_uq, kv_norm, w_ukv, rows, cols):
    b, n, _ = cq.shape
    q = (rms_norm(cq, q_norm) @ w_uq).reshape(b, n, MLA_HEADS, MLA_NOPE + MLA_ROPE).transpose(0, 2, 1, 3)
    kv = (rms_norm(ckv, kv_norm) @ w_ukv).reshape(b, n, MLA_HEADS, MLA_NOPE + MLA_V).transpose(0, 2, 1, 3)
    q_nope, q_rope = q[..., :MLA_NOPE], q[..., MLA_NOPE:]
    k_nope, v = kv[..., :MLA_NOPE], kv[..., MLA_NOPE:]
    if rows is not None:
        q_rope = axial_rope(q_rope, rows, cols)
        krope = axial_rope(krope, rows, cols)
    k_rope = jnp.broadcast_to(krope[:, None], (b, MLA_HEADS, n, MLA_ROPE))
    return (jnp.concatenate([q_nope, q_rope], -1), jnp.concatenate([k_nope, k_rope], -1), v)


def gqa_project(q, k, v, q_norm, k_norm, rows, cols):
    q = rms_norm(split_heads(q, GQA_HEADS), q_norm)
    k = rms_norm(split_heads(k, GQA_KV_HEADS), k_norm)
    v = split_heads(v, GQA_KV_HEADS)
    if rows is not None:
        q = axial_rope(q, rows, cols)
        k = axial_rope(k, rows, cols)
    return q, k, v


def centred_depthwise_conv(x, w, bias):
    pad_lo = CONV_W // 2
    y = lax.conv_general_dilated(x, w[:, None, :], window_strides=(1,),
                                 padding=[(pad_lo, CONV_W - 1 - pad_lo)],
                                 dimension_numbers=('NWC', 'WIO', 'NWC'),
                                 feature_group_count=x.shape[-1])
    return y + bias


def rglru_coeffs(x, w_a, b_a, w_x, b_x, lam):
    b, n, _ = x.shape
    xf = x.astype(jnp.float32)
    xb = xf.reshape(b, n, LRU_BLOCKS, LRU_BW)
    r = jax.nn.sigmoid(jnp.einsum('bnkc,kcd->bnkd', xb, w_a.astype(jnp.float32)).reshape(b, n, LRU_W) + b_a)
    i = jax.nn.sigmoid(jnp.einsum('bnkc,kcd->bnkd', xb, w_x.astype(jnp.float32)).reshape(b, n, LRU_W) + b_x)
    log_a = -LRU_C * r * jax.nn.softplus(-lam.astype(jnp.float32))
    a = jnp.exp(log_a)
    u = jnp.sqrt(-jnp.expm1(2.0 * log_a)) * (i * xf)
    return a, u


def linear_scan(a, u, h0, reverse):
    idx = -1 if reverse else 0
    u = u.at[:, idx].add(a[:, idx] * h0)

    def combine(lhs, rhs):
        a1, u1 = lhs
        a2, u2 = rhs
        return a1 * a2, a2 * u1 + u2

    _, h = lax.associative_scan(combine, (a, u), reverse=reverse, axis=1)
    return h


def rglru_mixer(x_lat, x_ctx, conv_w, conv_b, w_a, b_a, w_x, b_x, lam, need_ctx):
    xc = centred_depthwise_conv(x_ctx, conv_w, conv_b)
    xl = centred_depthwise_conv(x_lat, conv_w, conv_b)
    h_lat, h_ctx = [], []
    for d in range(2):
        reverse = d == 1
        end = 0 if reverse else -1
        a, u = rglru_coeffs(xc, w_a[d], b_a[d], w_x[d], b_x[d], lam[d])
        hc = linear_scan(a, u, jnp.zeros_like(a[:, 0]), reverse)
        a, u = rglru_coeffs(xl, w_a[d], b_a[d], w_x[d], b_x[d], lam[d])
        h_lat.append(linear_scan(a, u, hc[:, end], reverse))
        h_ctx.append(hc)
    y_lat = (h_lat[0] + h_lat[1]).astype(x_lat.dtype)
    y_ctx = (h_ctx[0] + h_ctx[1]).astype(x_ctx.dtype) if need_ctx else None
    return y_lat, y_ctx


def modulation(cond, w_mod, b_mod):
    mod = jax.nn.silu(cond) @ w_mod + b_mod
    return jnp.split(mod, 3, axis=-1)


def merge_branches(ys, z, m, w_branch, w_out):
    zs = jnp.split(z, N_BRANCH, axis=-1)
    ms = jnp.split(m, N_BRANCH, axis=-1)
    acc = sum(jax.nn.sigmoid(ms[i]) * ((ys[i] * jax.nn.silu(zs[i])) @ w_branch[i]) for i in range(N_BRANCH))
    return acc @ w_out


def hybrid_layer(x, ctx, c, c_ctx, w_mod, b_mod, w_in, na_rel_bias, mla_q_norm, mla_w_uq,
                 mla_kv_norm, mla_w_ukv, lru_conv_w, lru_conv_b, lru_w_a, lru_b_a, lru_w_x,
                 lru_b_x, lru_lambda, gqa_q_norm, gqa_k_norm, w_branch, w_out, ln_g, ln_b, need_ctx):
    n = x.shape[1]
    t = jnp.arange(n, dtype=jnp.int32)
    rows = (t // GRID_W).astype(jnp.float32)
    cols = (t % GRID_W).astype(jnp.float32)

    shift, scale, gate = modulation(c[:, None, :], w_mod, b_mod)
    shift_c, scale_c, gate_c = modulation(c_ctx, w_mod, b_mod)
    p_lat = (layer_norm(x) * (1.0 + scale) + shift) @ w_in
    w_in_ctx = w_in if need_ctx else w_in[:, :MIX_COLS]
    p_ctx = (layer_norm(ctx) * (1.0 + scale_c) + shift_c) @ w_in_ctx
    lat = split_cols(p_lat[..., :MIX_COLS], MIX_SPLITS)
    cx = split_cols(p_ctx[..., :MIX_COLS], MIX_SPLITS)

    q, k, v = (split_heads(tt, NA_HEADS) for tt in lat[0:3])
    qc, kc, vc = (split_heads(tt, NA_HEADS) for tt in cx[0:3])
    ya = merge_heads(neighbourhood_attention(q, k, v, kc, vc, na_rel_bias))
    ya_c = merge_heads(attend(qc, kc, vc, NA_SCALE)) if need_ctx else None

    q, k, v = mla_project(lat[3], lat[4], lat[5], mla_q_norm, mla_w_uq, mla_kv_norm, mla_w_ukv, rows, cols)
    qc, kc, vc = mla_project(cx[3], cx[4], cx[5], mla_q_norm, mla_w_uq, mla_kv_norm, mla_w_ukv, None, None)
    yb = merge_heads(prefix_attention(q, k, v, kc, vc, MLA_SCALE))
    yb_c = merge_heads(attend(qc, kc, vc, MLA_SCALE)) if need_ctx else None

    yc, yc_c = rglru_mixer(lat[6], cx[6], lru_conv_w, lru_conv_b, lru_w_a, lru_b_a,
                           lru_w_x, lru_b_x, lru_lambda, need_ctx)

    q, k, v = gqa_project(lat[7], lat[8], lat[9], gqa_q_norm, gqa_k_norm, rows, cols)
    qc, kc, vc = gqa_project(cx[7], cx[8], cx[9], gqa_q_norm, gqa_k_norm, None, None)
    yd = merge_heads(prefix_attention(q, k, v, kc, vc, GQA_SCALE))
    yd_c = merge_heads(attend(qc, kc, vc, GQA_SCALE)) if need_ctx else None

    z_lat = p_lat[..., MIX_COLS:MIX_COLS + SILU_COLS]
    m_lat = p_lat[..., MIX_COLS + SILU_COLS:]
    out = merge_branches([ya, yb, yc, yd], z_lat, m_lat, w_branch, w_out)
    x_new = layer_norm(DEEPNORM_ALPHA * x + gate * out) * ln_g + ln_b
    if need_ctx:
        z_c = p_ctx[..., MIX_COLS:MIX_COLS + SILU_COLS]
        m_c = p_ctx[..., MIX_COLS + SILU_COLS:]
        out_c = merge_branches([ya_c, yb_c, yc_c, yd_c], z_c, m_c, w_branch, w_out)
        ctx = layer_norm(DEEPNORM_ALPHA * ctx + gate_c * out_c) * ln_g + ln_b
    return x_new, ctx


def setup_inputs(seed: int = 0) -> dict:
    key = jax.random.key(seed)
    ks = jax.random.split(key, 32)
    f32 = jnp.float32
    L = DEPTH

    def nrm(k, shape, s):
        return jax.random.normal(k, shape, f32) * s

    lam_u = jax.random.uniform(ks[17], (L, 2, LRU_W), f32, 0.9, 0.999)
    a_base = lam_u ** (1.0 / LRU_C)
    lru_lambda = jnp.log(a_base) - jnp.log1p(-a_base)
    return {
        "x": nrm(ks[0], (BATCH, SEQ, D_MODEL), 1.0),
        "c": nrm(ks[1], (BATCH, D_MODEL), 1.0),
        "ctx": nrm(ks[2], (BATCH, CTX_LEN, D_MODEL), 1.0),
        "c_ctx": nrm(ks[3], (D_MODEL,), 1.0),
        "w_mod": nrm(ks[4], (L, D_MODEL, 3 * D_MODEL), 0.5 * D_MODEL ** -0.5),
        "b_mod": nrm(ks[5], (L, 3 * D_MODEL), 0.01),
        "w_in": nrm(ks[6], (L, D_MODEL, N_IN), D_MODEL ** -0.5),
        "na_rel_bias": nrm(ks[7], (L, NA_HEADS, 2 * WIN_H - 1, 2 * WIN_W - 1), 0.1),
        "mla_q_norm": 1.0 + nrm(ks[8], (L, MLA_Q_LORA), 0.01),
        "mla_w_uq": nrm(ks[9], (L, MLA_Q_LORA, MLA_HEADS * (MLA_NOPE + MLA_ROPE)), MLA_Q_LORA ** -0.5),
        "mla_kv_norm": 1.0 + nrm(ks[10], (L, MLA_KV_LORA), 0.01),
        "mla_w_ukv": nrm(ks[11], (L, MLA_KV_LORA, MLA_HEADS * (MLA_NOPE + MLA_V)), MLA_KV_LORA ** -0.5),
        "lru_conv_w": nrm(ks[12], (L, CONV_W, LRU_W), CONV_W ** -0.5),
        "lru_conv_b": nrm(ks[13], (L, LRU_W), 0.01),
        "lru_w_a": nrm(ks[14], (L, 2, LRU_BLOCKS, LRU_BW, LRU_BW), LRU_BW ** -0.5),
        "lru_b_a": nrm(ks[15], (L, 2, LRU_W), 0.01),
        "lru_w_x": nrm(ks[16], (L, 2, LRU_BLOCKS, LRU_BW, LRU_BW), LRU_BW ** -0.5),
        "lru_b_x": nrm(ks[18], (L, 2, LRU_W), 0.01),
        "lru_lambda": lru_lambda,
        "gqa_q_norm": 1.0 + nrm(ks[19], (L, HEAD_DIM), 0.01),
        "gqa_k_norm": 1.0 + nrm(ks[20], (L, HEAD_DIM), 0.01),
        "w_branch": nrm(ks[21], (L, N_BRANCH, BRANCH_W, D_MODEL), DEEPNORM_BETA * BRANCH_W ** -0.5),
        "w_out": nrm(ks[22], (L, D_MODEL, D_MODEL), DEEPNORM_BETA * D_MODEL ** -0.5),
        "ln_g": 1.0 + nrm(ks[23], (L, D_MODEL), 0.01),
        "ln_b": nrm(ks[24], (L, D_MODEL), 0.01),
    }


def reference(x, c, ctx, c_ctx, w_mod, b_mod, w_in, na_rel_bias, mla_q_norm, mla_w_uq,
              mla_kv_norm, mla_w_ukv, lru_conv_w, lru_conv_b, lru_w_a, lru_b_a, lru_w_x,
              lru_b_x, lru_lambda, gqa_q_norm, gqa_k_norm, w_branch, w_out, ln_g, ln_b):
    for l in range(DEPTH):
        need_ctx = l < DEPTH - 1
        x, ctx = hybrid_layer(x, ctx, c, c_ctx, w_mod[l], b_mod[l], w_in[l], na_rel_bias[l],
                              mla_q_norm[l], mla_w_uq[l], mla_kv_norm[l], mla_w_ukv[l],
                              lru_conv_w[l], lru_conv_b[l], lru_w_a[l], lru_b_a[l], lru_w_x[l],
                              lru_b_x[l], lru_lambda[l], gqa_q_norm[l], gqa_k_norm[l],
                              w_branch[l], w_out[l], ln_g[l], ln_b[l], need_ctx)
    return x
```

```python
import functools
import math

import numpy as np
import jax
import jax.numpy as jnp
from jax import lax
from jax.experimental import pallas as pl
from jax.experimental.pallas import tpu as pltpu

F32 = jnp.float32
BF16 = jnp.bfloat16

D_MODEL = 1024
DEPTH = 2
GRID_W = 64
BRANCH_W = 256
HEAD_DIM = 64
WIN_H = 8
WIN_W = 16
MLA_HEADS = 4
MLA_NOPE = 64
MLA_ROPE = 32
LRU_C = 8.0
CONV_W = 4
ROPE_THETA = 10000.0
EPS = 1e-6
LOG2E = 1.4426950408889634
NA_SCALE = HEAD_DIM ** -0.5
MLA_SCALE = (MLA_NOPE + MLA_ROPE) ** -0.5
GQA_SCALE = HEAD_DIM ** -0.5
DEEPNORM_ALPHA = (2 * DEPTH) ** 0.25

MIX_COLS = 1952
SILU_COLS = 1024
LANES = 128
NEG_BIG = -1e30
VMEM_LIMIT = 56 * 1024 * 1024

OFF_QA, OFF_KA, OFF_VA, OFF_CQ, OFF_CKV, OFF_LRU, OFF_QD, OFF_KD, OFF_VD, OFF_KR = (
    0, 256, 512, 768, 1024, 1152, 1408, 1664, 1792, 1920)
MIX_PAD = 2048


def _dot(a, b):
    return jnp.dot(a, b, preferred_element_type=F32)


def _dot_t(a, b):
    return lax.dot_general(a, b, (((1,), (1,)), ((), ())), preferred_element_type=F32)


def _layer_norm(x):
    mu = jnp.mean(x, axis=-1, keepdims=True)
    xc = x - mu
    var = jnp.mean(xc * xc, axis=-1, keepdims=True)
    return xc * lax.rsqrt(var + EPS)


def _rms(x, g):
    return x * lax.rsqrt(jnp.mean(x * x, axis=-1, keepdims=True) + EPS) * g


def _params(n_grid):
    return pltpu.CompilerParams(dimension_semantics=("arbitrary",) * n_grid,
                                vmem_limit_bytes=VMEM_LIMIT)


def _const_spec(shape):
    zeros = (0,) * len(shape)
    return pl.BlockSpec(shape, lambda *_: zeros)


def _mod_kernel(c_ref, w_ref, b_ref, o_ref):
    c = c_ref[...]
    s = (c * jax.nn.sigmoid(c)).astype(BF16)
    o_ref[0] = _dot(s, w_ref[0].astype(BF16)) + b_ref[0]


def _modulation(c_all, w_mod, b_mod):
    n_l, _, n_out = w_mod.shape
    rows = c_all.shape[0]
    bn = 512
    return pl.pallas_call(
        _mod_kernel,
        grid=(n_l, n_out // bn),
        in_specs=[pl.BlockSpec((rows, D_MODEL), lambda l, n: (0, 0)),
                  pl.BlockSpec((1, D_MODEL, bn), lambda l, n: (l, 0, n)),
                  pl.BlockSpec((1, 1, bn), lambda l, n: (l, 0, n))],
        out_specs=pl.BlockSpec((1, rows, bn), lambda l, n: (l, 0, n)),
        out_shape=jax.ShapeDtypeStruct((n_l, rows, n_out), F32),
        compiler_params=_params(2),
        name="modulation",
    )(c_all, w_mod, b_mod.reshape(n_l, 1, n_out))


def _pair_select(lane, shift):
    lane_up = pltpu.roll(lane, LANES - shift, 1)
    want = jnp.where((lane & (2 * shift - 1)) < shift, lane + shift, lane - shift)
    return lane_up == want


def _rope(x, cos, sin, shift, sel):
    partner = jnp.where(sel, pltpu.roll(x, LANES - shift, 1), pltpu.roll(x, shift, 1))
    return x * cos + partner * sin


def _group_sumsq(x, g):
    sq = x * x
    hi = sq.astype(BF16)
    lo = (sq - hi.astype(F32)).astype(BF16)
    return _dot(hi, g) + _dot(lo, g)


def _inproj_kernel(x_ref, mod_ref, wmix_ref, wuq_ref, wuk_ref, wuv_ref, pk_ref, g_ref,
                   gq_ref, gkv_ref, gdq_ref, gdk_ref,
                   cosd_ref, sind_ref, cosb_ref, sinb_ref, cosk_ref, sink_ref,
                   qa_o, ka_o, va_o, qb_o, kb_o, vb_o, xc_o, qd_o, kd_o, vd_o):
    x = x_ref[0]
    shift = mod_ref[0, 0:1, :]
    scale = mod_ref[0, 1:2, :]
    xm = (_layer_norm(x) * (1.0 + scale) + shift).astype(BF16)
    p = _dot(xm, wmix_ref[...])
    t = x.shape[0]
    lane = lax.broadcasted_iota(jnp.int32, (t, LANES), 1)
    sel16 = _pair_select(lane, 16)
    sel8 = _pair_select(lane, 8)

    qa_o[0] = (p[:, OFF_QA:OFF_QA + 256] * (NA_SCALE * LOG2E)).astype(BF16)
    ka_o[0] = p[:, OFF_KA:OFF_KA + 256].astype(BF16)
    va_o[0] = p[:, OFF_VA:OFF_VA + 256].astype(BF16)

    cqn = _rms(p[:, OFF_CQ:OFF_CQ + 256], gq_ref[...]).astype(BF16)
    qb = _dot(cqn, wuq_ref[...])
    cosb, sinb = cosb_ref[...], sinb_ref[...]
    for h in range(MLA_HEADS):
        blk = _rope(qb[:, LANES * h:LANES * (h + 1)], cosb, sinb, 8, sel8)
        qb_o[0, :, LANES * h:LANES * (h + 1)] = (blk * (MLA_SCALE * LOG2E)).astype(BF16)
    ckvn = _rms(p[:, OFF_CKV:OFF_CKV + 128], gkv_ref[...]).astype(BF16)
    kr = _rope(p[:, OFF_KR:OFF_KR + 128], cosk_ref[...], sink_ref[...], 8, sel8).astype(BF16)
    kb_o[0] = (_dot(ckvn, wuk_ref[...]) + _dot(kr, pk_ref[...])).astype(BF16)
    vb_o[0] = _dot(ckvn, wuv_ref[...]).astype(BF16)

    xc_o[0] = p[:, OFF_LRU:OFF_LRU + 256]

    cosd, sind = cosd_ref[...], sind_ref[...]
    qd = p[:, OFF_QD:OFF_QD + 256]
    qd = qd * lax.rsqrt(_group_sumsq(qd, g_ref[...]) * (1.0 / HEAD_DIM) + EPS) * gdq_ref[...]
    for blk_i in range(2):
        blk = _rope(qd[:, LANES * blk_i:LANES * (blk_i + 1)], cosd, sind, 16, sel16)
        qd_o[0, :, LANES * blk_i:LANES * (blk_i + 1)] = (blk * (GQA_SCALE * LOG2E)).astype(BF16)
    kd = p[:, OFF_KD:OFF_KD + 128]
    kd = kd * lax.rsqrt(_group_sumsq(kd, g_ref[0:128, 0:128]) * (1.0 / HEAD_DIM) + EPS) * gdk_ref[...]
    kd_o[0] = _rope(kd, cosd, sind, 16, sel16).astype(BF16)
    vd_o[0] = p[:, OFF_VD:OFF_VD + 128].astype(BF16)


def _inproj(x, mod, weights, tables, tile):
    b, n, _ = x.shape
    nt = n // tile
    wspecs = [_const_spec(w.shape) for w in weights]
    tspecs = [pl.BlockSpec((tile, LANES), lambda t, bb: (t, 0)) for _ in tables]
    widths = (256, 256, 256, 512, 512, 256, 256, 256, 128, 128)
    dtypes = (BF16, BF16, BF16, BF16, BF16, BF16, F32, BF16, BF16, BF16)
    return pl.pallas_call(
        _inproj_kernel,
        grid=(nt, b),
        in_specs=[pl.BlockSpec((1, tile, D_MODEL), lambda t, bb: (bb, t, 0)),
                  pl.BlockSpec((1, 3, D_MODEL), lambda t, bb: (bb, 0, 0))] + wspecs + tspecs,
        out_specs=[pl.BlockSpec((1, tile, w), lambda t, bb: (bb, t, 0)) for w in widths],
        out_shape=[jax.ShapeDtypeStruct((b, n, w), dt) for w, dt in zip(widths, dtypes)],
        compiler_params=_params(2),
        name="inproj",
    )(x, mod, *weights, *tables)


def _half_mask(lane, hh):
    return lane >= HEAD_DIM if hh else lane < HEAD_DIM


def _chunk_update(qh, kc, vc, m, l, acc):
    s = _dot_t(qh, kc)
    m_new = jnp.maximum(m, jnp.max(s, axis=-1, keepdims=True))
    alpha = jnp.exp2(m - m_new)
    pr = jnp.exp2(s - m_new)
    l = alpha * l + jnp.sum(pr, axis=-1, keepdims=True)
    acc = alpha * acc + _dot(pr.astype(BF16), vc)
    return m_new, l, acc


def _attn_kernel(*refs, masked, n_lat, chunk):
    if n_lat:
        q_ref, kl_ref, vl_ref, kc_ref, vc_ref, o_ref = refs
    else:
        q_ref, kc_ref, vc_ref, o_ref = refs
    tq = q_ref.shape[1]
    lane = lax.broadcasted_iota(jnp.int32, (tq, LANES), 1)
    outs = []
    for hh in range(2):
        if masked:
            qh = jnp.where(_half_mask(lane, hh), q_ref[0], jnp.zeros((), BF16))
            ksl = slice(None)
        else:
            qh = q_ref[0, :, LANES * hh:LANES * (hh + 1)]
            ksl = slice(LANES * hh, LANES * (hh + 1))
        carry = (jnp.full((tq, 1), NEG_BIG, F32), jnp.zeros((tq, 1), F32), jnp.zeros((tq, LANES), F32))
        if n_lat:
            def body(c, carry):
                rows = pl.ds(pl.multiple_of(c * chunk, chunk), chunk)
                return _chunk_update(qh, kl_ref[0, rows, ksl], vl_ref[0, rows, :], *carry)
            carry = lax.fori_loop(0, n_lat // chunk, body, carry)
        m, l, acc = _chunk_update(qh, kc_ref[0, :, ksl], vc_ref[0], *carry)
        outs.append(acc * (1.0 / l))
    o_ref[0] = jnp.where(lane < HEAD_DIM, outs[0], outs[1]).astype(BF16)


def _attention(q, k_lat, v_lat, k_ctx, v_ctx, *, mode, tq, chunk=1024):
    b, nq, _ = q.shape
    masked = mode != "B"
    qw = 256 if mode == "B" else LANES
    kw = 256 if mode == "B" else LANES
    kidx = (lambda bb, p, t: (bb, 0, 0)) if mode == "D" else (lambda bb, p, t: (bb, 0, p))
    n_ctx = k_ctx.shape[1]
    n_lat = 0 if k_lat is None else k_lat.shape[1]
    in_specs = [pl.BlockSpec((1, tq, qw), lambda bb, p, t: (bb, t, p))]
    args = [q]
    if n_lat:
        in_specs += [pl.BlockSpec((1, n_lat, kw), kidx), pl.BlockSpec((1, n_lat, LANES), kidx)]
        args += [k_lat, v_lat]
    in_specs += [pl.BlockSpec((1, n_ctx, kw), kidx), pl.BlockSpec((1, n_ctx, LANES), kidx)]
    args += [k_ctx, v_ctx]
    return pl.pallas_call(
        functools.partial(_attn_kernel, masked=masked, n_lat=n_lat, chunk=chunk),
        grid=(b, 2, nq // tq),
        in_specs=in_specs,
        out_specs=pl.BlockSpec((1, tq, LANES), lambda bb, p, t: (bb, t, p)),
        out_shape=jax.ShapeDtypeStruct((b, nq, 256), BF16),
        compiler_params=_params(3),
        name="attn_" + mode + ("_lat" if n_lat else "_ctx"),
    )(*args)


NA_QROWS = 8
NA_KROWS = 16
NA_TQ = NA_QROWS * GRID_W
NA_KBLK = 256


def _na_kernel(q_ref, k0, k1, k2, k3, v0, v1, v2, v3, kc_ref, vc_ref, bias_ref, o_ref):
    q = q_ref[0]
    lane = lax.broadcasted_iota(jnp.int32, (NA_TQ, LANES), 1)
    ks = (k0, k1, k2, k3)
    vs = (v0, v1, v2, v3)
    outs = []
    for hh in range(2):
        qh = jnp.where(_half_mask(lane, hh), q, jnp.zeros((), BF16))
        ss = [_dot_t(qh, kr[0]) + bias_ref[hh, 0, :, NA_KBLK * i:NA_KBLK * (i + 1)]
              for i, kr in enumerate(ks)]
        ss.append(_dot_t(qh, kc_ref[0]))
        mx = ss[0]
        for s in ss[1:]:
            mx = jnp.maximum(mx, s)
        m = jnp.max(mx, axis=-1, keepdims=True)
        ps = [jnp.exp2(s - m) for s in ss]
        tot = ps[0]
        for pr in ps[1:]:
            tot = tot + pr
        l = jnp.sum(tot, axis=-1, keepdims=True)
        o = _dot(ps[4].astype(BF16), vc_ref[0])
        for pr, vr in zip(ps[:4], vs):
            o = o + _dot(pr.astype(BF16), vr[0])
        outs.append(o * (1.0 / l))
    o_ref[0] = jnp.where(lane < HEAD_DIM, outs[0], outs[1]).astype(BF16)


def _na_bias(rel_bias):
    n_rows = 64
    qi = np.arange(NA_TQ)
    ki = np.arange(NA_KROWS * GRID_W)
    qc, kc = qi % GRID_W, ki % GRID_W
    cs = np.clip(qc - WIN_W // 2, 0, GRID_W - WIN_W)
    in_col = (kc[None, :] >= cs[:, None]) & (kc[None, :] < cs[:, None] + WIN_W)
    dc = kc[None, :] - qc[:, None]
    dr_all, ok_all = [], []
    for j in (0, 1, n_rows // NA_QROWS - 1):
        kb = int(np.clip(NA_QROWS * j - WIN_H // 2, 0, n_rows - NA_KROWS))
        qr = NA_QROWS * j + qi // GRID_W
        kr = kb + ki // GRID_W
        rs = np.clip(qr - WIN_H // 2, 0, n_rows - WIN_H)
        in_row = (kr[None, :] >= rs[:, None]) & (kr[None, :] < rs[:, None] + WIN_H)
        dr_all.append(kr[None, :] - qr[:, None])
        ok_all.append(in_row & in_col)
    dr = np.clip(np.stack(dr_all) + WIN_H - 1, 0, 2 * WIN_H - 2)
    dcc = np.clip(np.broadcast_to(dc, dr.shape) + WIN_W - 1, 0, 2 * WIN_W - 2)
    ok = np.stack(ok_all)
    vals = rel_bias[:, dr, dcc] * LOG2E
    return jnp.where(ok[None], vals, NEG_BIG).astype(F32)


def _na_attention(q, k, v, k_ctx, v_ctx, bias):
    b, n, _ = q.shape
    nj = n // NA_TQ
    n_kblk = n // NA_KBLK
    n_ctx = k_ctx.shape[1]

    def kstart(j):
        return jnp.clip(2 * j - 1, 0, n_kblk - 4)

    def kspec(i):
        return pl.BlockSpec((1, NA_KBLK, LANES), lambda p, j, bb: (bb, kstart(j) + i, p))

    def variant(j):
        return jnp.where(j == 0, 0, jnp.where(j == nj - 1, 2, 1))

    return pl.pallas_call(
        _na_kernel,
        grid=(2, nj, b),
        in_specs=[pl.BlockSpec((1, NA_TQ, LANES), lambda p, j, bb: (bb, j, p))]
        + [kspec(i) for i in range(4)] + [kspec(i) for i in range(4)]
        + [pl.BlockSpec((1, n_ctx, LANES), lambda p, j, bb: (bb, 0, p)),
           pl.BlockSpec((1, n_ctx, LANES), lambda p, j, bb: (bb, 0, p)),
           pl.BlockSpec((2, 1, NA_TQ, NA_KROWS * GRID_W), lambda p, j, bb: (p, variant(j), 0, 0))],
        out_specs=pl.BlockSpec((1, NA_TQ, LANES), lambda p, j, bb: (bb, j, p)),
        out_shape=jax.ShapeDtypeStruct((b, n, 256), BF16),
        compiler_params=_params(3),
        name="na_lat",
    )(q, k, k, k, k, v, v, v, v, k_ctx, v_ctx, bias)


LRU_SEG = 8
LRU_PAD = 8
LRU_ROWS = 512
LRU_HALVES = 2


def _put(ref, rows, val):
    for hv in range(LRU_HALVES):
        ref[hv, rows, :] = val[:, LANES * hv:LANES * (hv + 1)]


def _get(ref, rows):
    return jnp.concatenate([ref[hv, rows, :] for hv in range(LRU_HALVES)], axis=-1)


def _lru_gates(xp, n, cw, cb, wa_ref, ba_ref, wx_ref, bx_ref, sp, a_refs, u_refs):
    r = min(LRU_ROWS, n)
    for c0 in range(0, n, r):
        conv = cb
        for tap in range(CONV_W):
            conv = conv + cw[tap:tap + 1, :] * xp[pl.ds(LRU_PAD + c0 + tap - CONV_W // 2, r), :]
        xb = conv.astype(BF16)
        for d in range(2):
            rg = jax.nn.sigmoid(_dot(xb, wa_ref[d]) + ba_ref[d:d + 1, :])
            ig = jax.nn.sigmoid(_dot(xb, wx_ref[d]) + bx_ref[d:d + 1, :])
            log_a = (-LRU_C) * rg * sp[d:d + 1, :]
            _put(a_refs[d], pl.ds(c0, r), jnp.exp(log_a))
            _put(u_refs[d], pl.ds(c0, r), jnp.sqrt(1.0 - jnp.exp(2.0 * log_a)) * (ig * conv))


def _lru_scan(n, af, uf, ab, ub, hf0, hb0):
    seg = n // LRU_SEG

    def body(i, carry):
        rf = pl.ds(i, LRU_SEG, stride=seg)
        rb = pl.ds(seg - 1 - i, LRU_SEG, stride=seg)
        out = []
        for hv in range(LRU_HALVES):
            hlf, cmf, hlb, cmb = carry[4 * hv:4 * hv + 4]
            a1, u1 = af[hv, rf, :], uf[hv, rf, :]
            a2, u2 = ab[hv, rb, :], ub[hv, rb, :]
            hlf = a1 * hlf + u1
            cmf = a1 * cmf
            hlb = a2 * hlb + u2
            cmb = a2 * cmb
            uf[hv, rf, :] = hlf
            af[hv, rf, :] = cmf
            ub[hv, rb, :] = hlb
            ab[hv, rb, :] = cmb
            out += [hlf, cmf, hlb, cmb]
        return tuple(out)

    zero = jnp.zeros((LRU_SEG, LANES), F32)
    one = jnp.ones((LRU_SEG, LANES), F32)
    lax.fori_loop(0, seg, body, (zero, one, zero, one) * LRU_HALVES)
    hf, hb = list(hf0), list(hb0)
    for s in range(LRU_SEG):
        rf = pl.ds(s * seg, seg)
        rb = pl.ds((LRU_SEG - 1 - s) * seg, seg)
        for hv in range(LRU_HALVES):
            h = uf[hv, rf, :] + af[hv, rf, :] * hf[hv]
            uf[hv, rf, :] = h
            hf[hv] = h[seg - 1:seg, :]
            h = ub[hv, rb, :] + ab[hv, rb, :] * hb[hv]
            ub[hv, rb, :] = h
            hb[hv] = h[0:1, :]
    return hf, hb


def _lru_kernel(xl_ref, xc_ref, cw_ref, cb_ref, wa_ref, ba_ref, wx_ref, bx_ref, lam_ref,
                yl_ref, yc_ref, xp, afl, ufl, abl, ubl, afc, ufc, abc, ubc):
    n_lat, w = xl_ref.shape[1], xl_ref.shape[2]
    n_ctx = xc_ref.shape[1]
    cw, cb = cw_ref[...], cb_ref[...]
    neg_lam = -lam_ref[...]
    sp = jnp.maximum(neg_lam, 0.0) + jnp.log1p(jnp.exp(-jnp.abs(neg_lam)))
    zpad = jnp.zeros((LRU_PAD, w), F32)

    xp[pl.ds(0, LRU_PAD), :] = zpad
    xp[pl.ds(LRU_PAD, n_ctx), :] = xc_ref[0]
    xp[pl.ds(LRU_PAD + n_ctx, LRU_PAD), :] = zpad
    _lru_gates(xp, n_ctx, cw, cb, wa_ref, ba_ref, wx_ref, bx_ref, sp, (afc, abc), (ufc, ubc))
    xp[pl.ds(LRU_PAD, n_lat), :] = xl_ref[0]
    xp[pl.ds(LRU_PAD + n_lat, LRU_PAD), :] = zpad
    _lru_gates(xp, n_lat, cw, cb, wa_ref, ba_ref, wx_ref, bx_ref, sp, (afl, abl), (ufl, ubl))

    h0 = [jnp.zeros((1, LANES), F32)] * LRU_HALVES
    hf, hb = _lru_scan(n_ctx, afc, ufc, abc, ubc, h0, h0)
    _lru_scan(n_lat, afl, ufl, abl, ubl, hf, hb)
    rows = pl.ds(0, n_ctx)
    yc_ref[0] = (_get(ufc, rows) + _get(ubc, rows)).astype(BF16)
    for c0 in range(0, n_lat, LRU_ROWS):
        rows = pl.ds(c0, LRU_ROWS)
        yl_ref[0, rows, :] = (_get(ufl, rows) + _get(ubl, rows)).astype(BF16)


def _lru(x_lat, x_ctx, conv_w, conv_b, wa, ba, wx, bx, lam):
    b, n_lat, w = x_lat.shape
    n_ctx = x_ctx.shape[1]
    consts = (conv_w, conv_b, wa, ba, wx, bx, lam)
    big = lambda: pltpu.VMEM((LRU_HALVES, n_lat, LANES), F32)
    small = lambda: pltpu.VMEM((LRU_HALVES, n_ctx, LANES), F32)
    return pl.pallas_call(
        _lru_kernel,
        grid=(b,),
        in_specs=[pl.BlockSpec((1, n_lat, w), lambda bb: (bb, 0, 0)),
                  pl.BlockSpec((1, n_ctx, w), lambda bb: (bb, 0, 0))]
        + [_const_spec(c.shape) for c in consts],
        out_specs=[pl.BlockSpec((1, n_lat, w), lambda bb: (bb, 0, 0)),
                   pl.BlockSpec((1, n_ctx, w), lambda bb: (bb, 0, 0))],
        out_shape=[jax.ShapeDtypeStruct((b, n_lat, w), BF16),
                   jax.ShapeDtypeStruct((b, n_ctx, w), BF16)],
        scratch_shapes=[pltpu.VMEM((n_lat + 2 * LRU_PAD, w), F32),
                        big(), big(), big(), big(), small(), small(), small(), small()],
        compiler_params=_params(1),
        name="rglru",
    )(x_lat, x_ctx, *consts)


def _merge_kernel(x_ref, mod_ref, ya_ref, yb_ref, yc_ref, yd_ref, wz_ref, wm_ref, wb_ref, wo_ref,
                  lng_ref, lnb_ref, o_ref):
    x = x_ref[0]
    shift = mod_ref[0, 0:1, :]
    scale = mod_ref[0, 1:2, :]
    gate = mod_ref[0, 2:3, :]
    xm = (_layer_norm(x) * (1.0 + scale) + shift).astype(BF16)
    z = _dot(xm, wz_ref[...])
    acc = None
    for i, y_ref in enumerate((ya_ref, yb_ref, yc_ref, yd_ref)):
        zi = z[:, BRANCH_W * i:BRANCH_W * (i + 1)]
        g = (y_ref[0].astype(F32) * (zi * jax.nn.sigmoid(zi))).astype(BF16)
        t = _dot(g, wb_ref[i])
        mi = _dot(xm, wm_ref[:, D_MODEL * i:D_MODEL * (i + 1)])
        term = jax.nn.sigmoid(mi) * t
        acc = term if acc is None else acc + term
    out = _dot(acc.astype(BF16), wo_ref[...])
    o_ref[0] = _layer_norm(DEEPNORM_ALPHA * x + gate * out) * lng_ref[...] + lnb_ref[...]


def _merge(x, mod, ys, wz, wm, wb, wo, ln_g, ln_b, tile):
    b, n, _ = x.shape
    consts = (wz, wm, wb, wo, ln_g, ln_b)
    tok = lambda w: pl.BlockSpec((1, tile, w), lambda bb, t: (bb, t, 0))
    return pl.pallas_call(
        _merge_kernel,
        grid=(b, n // tile),
        in_specs=[tok(D_MODEL), pl.BlockSpec((1, 3, D_MODEL), lambda bb, t: (bb, 0, 0))]
        + [tok(BRANCH_W) for _ in ys] + [_const_spec(c.shape) for c in consts],
        out_specs=tok(D_MODEL),
        out_shape=jax.ShapeDtypeStruct((b, n, D_MODEL), F32),
        compiler_params=_params(2),
        name="merge",
    )(x, mod, *ys, *consts)


def _rope_tables(n):
    t = jnp.arange(n, dtype=jnp.int32)
    rows = (t // GRID_W).astype(F32)[:, None]
    cols = (t % GRID_W).astype(F32)[:, None]

    def table(half, offset):
        inv = ROPE_THETA ** (-jnp.arange(0, 2 * half, 2, dtype=F32) / (2 * half))
        cos = jnp.ones((n, LANES), F32)
        sin = jnp.zeros((n, LANES), F32)
        for g, pos in enumerate((rows, cols)):
            ang = pos * inv[None, :]
            c, s = jnp.cos(ang), jnp.sin(ang)
            base = offset + 2 * half * g
            cos = cos.at[:, base:base + 2 * half].set(jnp.concatenate([c, c], -1))
            sin = sin.at[:, base:base + 2 * half].set(jnp.concatenate([-s, s], -1))
        return cos, sin

    cd, sd = table(16, 0)
    cosd = jnp.concatenate([cd[:, :64], cd[:, :64]], -1)
    sind = jnp.concatenate([sd[:, :64], sd[:, :64]], -1)
    cosb, sinb = table(8, MLA_NOPE)
    cosk, sink = table(8, 0)
    return cosd, sind, cosb, sinb, cosk, sink


def _identity_tables(n):
    one, zero = jnp.ones((n, LANES), F32), jnp.zeros((n, LANES), F32)
    return one, zero, one, zero, one, zero


def _layer_weights(w_in, mla_q_norm, mla_w_uq, mla_kv_norm, mla_w_ukv, gqa_q_norm, gqa_k_norm, w_branch, w_out):
    wi = w_in[:, :MIX_COLS]
    qa, ka, va = wi[:, 0:256], wi[:, 256:512], wi[:, 512:768]
    cq, ckv, kr = wi[:, 768:1024], wi[:, 1024:1152], wi[:, 1152:1184]
    lru, qd, kd, vd = wi[:, 1184:1440], wi[:, 1440:1696], wi[:, 1696:1824], wi[:, 1824:1952]
    head_order = jnp.array([0, 2, 1, 3])
    qd = qd.reshape(D_MODEL, 4, HEAD_DIM)[:, head_order].reshape(D_MODEL, 256)
    wmix = jnp.concatenate([qa, ka, va, cq, ckv, lru, qd, kd, vd, kr,
                            jnp.zeros((D_MODEL, MIX_PAD - MIX_COLS), F32)], -1).astype(BF16)
    hq = MLA_NOPE + MLA_ROPE
    wuq = jnp.pad(mla_w_uq.reshape(256, MLA_HEADS, hq), ((0, 0), (0, 0), (0, LANES - hq)))
    wuq = wuq.reshape(256, MLA_HEADS * LANES).astype(BF16)
    ukv = mla_w_ukv.reshape(128, MLA_HEADS, 128)
    wuk = jnp.pad(ukv[:, :, :MLA_NOPE], ((0, 0), (0, 0), (0, LANES - MLA_NOPE)))
    wuk = wuk.reshape(128, MLA_HEADS * LANES).astype(BF16)
    wuv = ukv[:, :, MLA_NOPE:].reshape(128, 256).astype(BF16)
    pk = np.zeros((LANES, MLA_HEADS * LANES), np.float32)
    for h in range(MLA_HEADS):
        pk[np.arange(MLA_ROPE), LANES * h + MLA_NOPE + np.arange(MLA_ROPE)] = 1.0
    grp = np.kron(np.eye(4, dtype=np.float32), np.ones((HEAD_DIM, HEAD_DIM), np.float32))
    inproj_w = (wmix, wuq, wuk, wuv, jnp.asarray(pk, BF16), jnp.asarray(grp, BF16),
                mla_q_norm.reshape(1, 256), mla_kv_norm.reshape(1, 128),
                jnp.tile(gqa_q_norm, 4).reshape(1, 256), jnp.tile(gqa_k_norm, 2).reshape(1, 128))
    wz = w_in[:, MIX_COLS:MIX_COLS + SILU_COLS]
    wz_d = wz[:, 3 * BRANCH_W:].reshape(D_MODEL, 4, HEAD_DIM)[:, head_order].reshape(D_MODEL, BRANCH_W)
    wz = jnp.concatenate([wz[:, :3 * BRANCH_W], wz_d], -1).astype(BF16)
    wm = w_in[:, MIX_COLS + SILU_COLS:].astype(BF16)
    wb_d = w_branch[3].reshape(4, HEAD_DIM, D_MODEL)[head_order].reshape(BRANCH_W, D_MODEL)
    wb = jnp.concatenate([w_branch[:3], wb_d[None]], 0).astype(BF16)
    return inproj_w, (wz, wm, wb, w_out.astype(BF16))


def _block_diag(w):
    eye = jnp.eye(4, dtype=w.dtype)
    return jnp.einsum("dkce,kj->dkcje", w, eye).reshape(2, 256, 256)


def kernel(x, c, ctx, c_ctx, w_mod, b_mod, w_in, na_rel_bias, mla_q_norm, mla_w_uq, mla_kv_norm, mla_w_ukv,
           lru_conv_w, lru_conv_b, lru_w_a, lru_b_a, lru_w_x, lru_b_x, lru_lambda, gqa_q_norm, gqa_k_norm,
           w_branch, w_out, ln_g, ln_b):
    b, n, _ = x.shape
    n_ctx = ctx.shape[1]
    rows = 8 * ((b + 1 + 7) // 8)
    c_all = jnp.zeros((rows, D_MODEL), F32).at[:b].set(c).at[b].set(c_ctx)
    mod_all = _modulation(c_all, w_mod, b_mod)
    lat_tables = _rope_tables(n)
    ctx_tables = _identity_tables(n_ctx)

    for l in range(DEPTH):
        need_ctx = l < DEPTH - 1
        mod_lat = mod_all[l, :b].reshape(b, 3, D_MODEL)
        mod_ctx = jnp.broadcast_to(mod_all[l, b].reshape(1, 3, D_MODEL), (b, 3, D_MODEL))
        inproj_w, merge_w = _layer_weights(w_in[l], mla_q_norm[l], mla_w_uq[l], mla_kv_norm[l], mla_w_ukv[l],
                                           gqa_q_norm[l], gqa_k_norm[l], w_branch[l], w_out[l])
        qa, ka, va, qb, kb, vb, xr, qd, kd, vd = _inproj(x, mod_lat, inproj_w, lat_tables, 512)
        qa_c, ka_c, va_c, qb_c, kb_c, vb_c, xr_c, qd_c, kd_c, vd_c = _inproj(
            ctx, mod_ctx, inproj_w, ctx_tables, n_ctx)

        ya = _na_attention(qa, ka, va, ka_c, va_c, _na_bias(na_rel_bias[l]))
        yb = _attention(qb, kb, vb, kb_c, vb_c, mode="B", tq=512)
        yd = _attention(qd, kd, vd, kd_c, vd_c, mode="D", tq=512)
        yc, yc_c = _lru(xr, xr_c, lru_conv_w[l], lru_conv_b[l].reshape(1, -1),
                        _block_diag(lru_w_a[l]).astype(BF16), lru_b_a[l],
                        _block_diag(lru_w_x[l]).astype(BF16), lru_b_x[l], lru_lambda[l])
        tail = (*merge_w, ln_g[l].reshape(1, -1), ln_b[l].reshape(1, -1))
        x_new = _merge(x, mod_lat, (ya, yb, yc, yd), *tail, 512)
        if need_ctx:
            ya_c = _attention(qa_c, None, None, ka_c, va_c, mode="A", tq=n_ctx)
            yb_c = _attention(qb_c, None, None, kb_c, vb_c, mode="B", tq=n_ctx)
            yd_c = _attention(qd_c, None, None, kd_c, vd_c, mode="D", tq=n_ctx)
            ctx = _merge(ctx, mod_ctx, (ya_c, yb_c, yc_c, yd_c), *tail, n_ctx)
        x = x_new
    return x
```

```python
import functools
import math

import numpy as np
import jax
import jax.numpy as jnp
from jax import lax
from jax.experimental import pallas as pl
from jax.experimental.pallas import tpu as pltpu

F32 = jnp.float32
BF16 = jnp.bfloat16

D_MODEL = 1024
DEPTH = 2
GRID_W = 64
BRANCH_W = 256
HEAD_DIM = 64
WIN_H = 8
WIN_W = 16
MLA_HEADS = 4
MLA_NOPE = 64
MLA_ROPE = 32
LRU_C = 8.0
CONV_W = 4
ROPE_THETA = 10000.0
EPS = 1e-6
LOG2E = 1.4426950408889634
NA_SCALE = HEAD_DIM ** -0.5
MLA_SCALE = (MLA_NOPE + MLA_ROPE) ** -0.5
GQA_SCALE = HEAD_DIM ** -0.5
DEEPNORM_ALPHA = (2 * DEPTH) ** 0.25

MIX_COLS = 1952
SILU_COLS = 1024
LANES = 128
NEG_BIG = -1e30
VMEM_LIMIT = 56 * 1024 * 1024

OFF_QA, OFF_KA, OFF_VA, OFF_CQ, OFF_CKV, OFF_LRU, OFF_QD, OFF_KD, OFF_VD, OFF_KR = (
    0, 256, 512, 768, 1024, 1152, 1408, 1664, 1792, 1920)
MIX_PAD = 2048


def _dot(a, b):
    return jnp.dot(a, b, preferred_element_type=F32)


def _dot_t(a, b):
    return lax.dot_general(a, b, (((1,), (1,)), ((), ())), preferred_element_type=F32)


def _layer_norm(x):
    mu = jnp.mean(x, axis=-1, keepdims=True)
    xc = x - mu
    var = jnp.mean(xc * xc, axis=-1, keepdims=True)
    return xc * lax.rsqrt(var + EPS)


def _rms(x, g):
    return x * lax.rsqrt(jnp.mean(x * x, axis=-1, keepdims=True) + EPS) * g


def _params(n_grid):
    return pltpu.CompilerParams(dimension_semantics=("arbitrary",) * n_grid,
                                vmem_limit_bytes=VMEM_LIMIT)


def _const_spec(shape):
    zeros = (0,) * len(shape)
    return pl.BlockSpec(shape, lambda *_: zeros)


def _mod_kernel(c_ref, w_ref, b_ref, o_ref):
    c = c_ref[...]
    s = (c * jax.nn.sigmoid(c)).astype(BF16)
    o_ref[0] = _dot(s, w_ref[0].astype(BF16)) + b_ref[0]


def _modulation(c_all, w_mod, b_mod):
    n_l, _, n_out = w_mod.shape
    rows = c_all.shape[0]
    bn = 512
    return pl.pallas_call(
        _mod_kernel,
        grid=(n_l, n_out // bn),
        in_specs=[pl.BlockSpec((rows, D_MODEL), lambda l, n: (0, 0)),
                  pl.BlockSpec((1, D_MODEL, bn), lambda l, n: (l, 0, n)),
                  pl.BlockSpec((1, 1, bn), lambda l, n: (l, 0, n))],
        out_specs=pl.BlockSpec((1, rows, bn), lambda l, n: (l, 0, n)),
        out_shape=jax.ShapeDtypeStruct((n_l, rows, n_out), F32),
        compiler_params=_params(2),
        name="modulation",
    )(c_all, w_mod, b_mod.reshape(n_l, 1, n_out))


def _pair_select(lane, shift):
    lane_up = pltpu.roll(lane, LANES - shift, 1)
    want = jnp.where((lane & (2 * shift - 1)) < shift, lane + shift, lane - shift)
    return lane_up == want


def _rope(x, cos, sin, shift, sel):
    partner = jnp.where(sel, pltpu.roll(x, LANES - shift, 1), pltpu.roll(x, shift, 1))
    return x * cos + partner * sin


def _group_sumsq(x, g):
    sq = x * x
    hi = sq.astype(BF16)
    lo = (sq - hi.astype(F32)).astype(BF16)
    return _dot(hi, g) + _dot(lo, g)


def _inproj_kernel(x_ref, mod_ref, wmix_ref, wuq_ref, wuk_ref, wuv_ref, pk_ref, g_ref,
                   gq_ref, gkv_ref, gdq_ref, gdk_ref,
                   cosd_ref, sind_ref, cosb_ref, sinb_ref, cosk_ref, sink_ref,
                   qa_o, ka_o, va_o, qb_o, kb_o, vb_o, xc_o, qd_o, kd_o, vd_o):
    x = x_ref[0]
    shift = mod_ref[0, 0:1, :]
    scale = mod_ref[0, 1:2, :]
    xm = (_layer_norm(x) * (1.0 + scale) + shift).astype(BF16)
    p = _dot(xm, wmix_ref[...])
    t = x.shape[0]
    lane = lax.broadcasted_iota(jnp.int32, (t, LANES), 1)
    sel16 = _pair_select(lane, 16)
    sel8 = _pair_select(lane, 8)

    qa_o[0] = (p[:, OFF_QA:OFF_QA + 256] * (NA_SCALE * LOG2E)).astype(BF16)
    ka_o[0] = p[:, OFF_KA:OFF_KA + 256].astype(BF16)
    va_o[0] = p[:, OFF_VA:OFF_VA + 256].astype(BF16)

    cqn = _rms(p[:, OFF_CQ:OFF_CQ + 256], gq_ref[...]).astype(BF16)
    qb = _dot(cqn, wuq_ref[...])
    cosb, sinb = cosb_ref[...], sinb_ref[...]
    for h in range(MLA_HEADS):
        blk = _rope(qb[:, LANES * h:LANES * (h + 1)], cosb, sinb, 8, sel8)
        qb_o[0, :, LANES * h:LANES * (h + 1)] = (blk * (MLA_SCALE * LOG2E)).astype(BF16)
    ckvn = _rms(p[:, OFF_CKV:OFF_CKV + 128], gkv_ref[...]).astype(BF16)
    kr = _rope(p[:, OFF_KR:OFF_KR + 128], cosk_ref[...], sink_ref[...], 8, sel8).astype(BF16)
    kb_o[0] = (_dot(ckvn, wuk_ref[...]) + _dot(kr, pk_ref[...])).astype(BF16)
    vb_o[0] = _dot(ckvn, wuv_ref[...]).astype(BF16)

    xc_o[0] = p[:, OFF_LRU:OFF_LRU + 256]

    cosd, sind = cosd_ref[...], sind_ref[...]
    qd = p[:, OFF_QD:OFF_QD + 256]
    qd = qd * lax.rsqrt(_group_sumsq(qd, g_ref[...]) * (1.0 / HEAD_DIM) + EPS) * gdq_ref[...]
    for blk_i in range(2):
        blk = _rope(qd[:, LANES * blk_i:LANES * (blk_i + 1)], cosd, sind, 16, sel16)
        qd_o[0, :, LANES * blk_i:LANES * (blk_i + 1)] = (blk * (GQA_SCALE * LOG2E)).astype(BF16)
    kd = p[:, OFF_KD:OFF_KD + 128]
    kd = kd * lax.rsqrt(_group_sumsq(kd, g_ref[0:128, 0:128]) * (1.0 / HEAD_DIM) + EPS) * gdk_ref[...]
    kd_o[0] = _rope(kd, cosd, sind, 16, sel16).astype(BF16)
    vd_o[0] = p[:, OFF_VD:OFF_VD + 128].astype(BF16)


def _inproj(x, mod, weights, tables, tile):
    b, n, _ = x.shape
    nt = n // tile
    wspecs = [_const_spec(w.shape) for w in weights]
    tspecs = [pl.BlockSpec((tile, LANES), lambda t, bb: (t, 0)) for _ in tables]
    widths = (256, 256, 256, 512, 512, 256, 256, 256, 128, 128)
    dtypes = (BF16, BF16, BF16, BF16, BF16, BF16, F32, BF16, BF16, BF16)
    return pl.pallas_call(
        _inproj_kernel,
        grid=(nt, b),
        in_specs=[pl.BlockSpec((1, tile, D_MODEL), lambda t, bb: (bb, t, 0)),
                  pl.BlockSpec((1, 3, D_MODEL), lambda t, bb: (bb, 0, 0))] + wspecs + tspecs,
        out_specs=[pl.BlockSpec((1, tile, w), lambda t, bb: (bb, t, 0)) for w in widths],
        out_shape=[jax.ShapeDtypeStruct((b, n, w), dt) for w, dt in zip(widths, dtypes)],
        compiler_params=_params(2),
        name="inproj",
    )(x, mod, *weights, *tables)


def _half_mask(lane, hh):
    return lane >= HEAD_DIM if hh else lane < HEAD_DIM


def _chunk_update(qh, kc, vc, m, l, acc):
    s = _dot_t(qh, kc)
    m_new = jnp.maximum(m, jnp.max(s, axis=-1, keepdims=True))
    alpha = jnp.exp2(m - m_new)
    pr = jnp.exp2(s - m_new)
    l = alpha * l + jnp.sum(pr, axis=-1, keepdims=True)
    acc = alpha * acc + _dot(pr.astype(BF16), vc)
    return m_new, l, acc


def _attn_kernel(*refs, masked, n_lat, chunk):
    if n_lat:
        q_ref, kl_ref, vl_ref, kc_ref, vc_ref, o_ref = refs
    else:
        q_ref, kc_ref, vc_ref, o_ref = refs
    tq = q_ref.shape[1]
    lane = lax.broadcasted_iota(jnp.int32, (tq, LANES), 1)
    qs, ksls = [], []
    for hh in range(2):
        if masked:
            qs.append(jnp.where(_half_mask(lane, hh), q_ref[0], jnp.zeros((), BF16)))
            ksls.append(slice(None))
        else:
            qs.append(q_ref[0, :, LANES * hh:LANES * (hh + 1)])
            ksls.append(slice(LANES * hh, LANES * (hh + 1)))
    carries = [(jnp.full((tq, 1), NEG_BIG, F32), jnp.zeros((tq, 1), F32), jnp.zeros((tq, LANES), F32))
               for _ in range(2)]
    for c0 in range(0, n_lat, chunk):
        rows = pl.ds(c0, chunk)
        for hh in range(2):
            carries[hh] = _chunk_update(qs[hh], kl_ref[0, rows, ksls[hh]], vl_ref[0, rows, :], *carries[hh])
    outs = []
    for hh in range(2):
        m, l, acc = _chunk_update(qs[hh], kc_ref[0, :, ksls[hh]], vc_ref[0], *carries[hh])
        outs.append(acc * (1.0 / l))
    o_ref[0] = jnp.where(lane < HEAD_DIM, outs[0], outs[1]).astype(BF16)


def _attention(q, k_lat, v_lat, k_ctx, v_ctx, *, mode, tq, chunk=1024):
    b, nq, _ = q.shape
    masked = mode != "B"
    qw = 256 if mode == "B" else LANES
    kw = 256 if mode == "B" else LANES
    kidx = (lambda bb, p, t: (bb, 0, 0)) if mode == "D" else (lambda bb, p, t: (bb, 0, p))
    n_ctx = k_ctx.shape[1]
    n_lat = 0 if k_lat is None else k_lat.shape[1]
    in_specs = [pl.BlockSpec((1, tq, qw), lambda bb, p, t: (bb, t, p))]
    args = [q]
    if n_lat:
        in_specs += [pl.BlockSpec((1, n_lat, kw), kidx), pl.BlockSpec((1, n_lat, LANES), kidx)]
        args += [k_lat, v_lat]
    in_specs += [pl.BlockSpec((1, n_ctx, kw), kidx), pl.BlockSpec((1, n_ctx, LANES), kidx)]
    args += [k_ctx, v_ctx]
    return pl.pallas_call(
        functools.partial(_attn_kernel, masked=masked, n_lat=n_lat, chunk=chunk),
        grid=(b, 2, nq // tq),
        in_specs=in_specs,
        out_specs=pl.BlockSpec((1, tq, LANES), lambda bb, p, t: (bb, t, p)),
        out_shape=jax.ShapeDtypeStruct((b, nq, 256), BF16),
        compiler_params=_params(3),
        name="attn_" + mode + ("_lat" if n_lat else "_ctx"),
    )(*args)


NA_QROWS = 8
NA_KROWS = 16
NA_TQ = NA_QROWS * GRID_W
NA_KBLK = 256


def _na_kernel(q_ref, k0, k1, k2, k3, v0, v1, v2, v3, kc_ref, vc_ref, bias_ref, o_ref):
    q = q_ref[0]
    lane = lax.broadcasted_iota(jnp.int32, (NA_TQ, LANES), 1)
    ks = (k0, k1, k2, k3)
    vs = (v0, v1, v2, v3)
    outs = []
    for hh in range(2):
        qh = jnp.where(_half_mask(lane, hh), q, jnp.zeros((), BF16))
        ss = [_dot_t(qh, kr[0]) + bias_ref[hh, 0, :, NA_KBLK * i:NA_KBLK * (i + 1)]
              for i, kr in enumerate(ks)]
        ss.append(_dot_t(qh, kc_ref[0]))
        mx = ss[0]
        for s in ss[1:]:
            mx = jnp.maximum(mx, s)
        m = jnp.max(mx, axis=-1, keepdims=True)
        ps = [jnp.exp2(s - m) for s in ss]
        tot = ps[0]
        for pr in ps[1:]:
            tot = tot + pr
        l = jnp.sum(tot, axis=-1, keepdims=True)
        o = _dot(ps[4].astype(BF16), vc_ref[0])
        for pr, vr in zip(ps[:4], vs):
            o = o + _dot(pr.astype(BF16), vr[0])
        outs.append(o * (1.0 / l))
    o_ref[0] = jnp.where(lane < HEAD_DIM, outs[0], outs[1]).astype(BF16)


def _na_bias(rel_bias):
    h, n_dr, n_dc = rel_bias.shape
    n_rows = 64
    w2 = 2 * GRID_W
    pad_lo = GRID_W - WIN_W
    p = jnp.pad(rel_bias * LOG2E, ((0, 0), (0, 0), (pad_lo, w2 - pad_lo - n_dc)))
    skew = jnp.broadcast_to(p[:, :, None, :], (h, n_dr, GRID_W, w2)).reshape(h, n_dr, GRID_W * w2)
    skew = skew[:, :, :GRID_W * (w2 - 1)].reshape(h, n_dr, GRID_W, w2 - 1)
    tiles = skew[..., GRID_W - 1:]
    qc = np.arange(GRID_W)
    cs = np.clip(qc - WIN_W // 2, 0, GRID_W - WIN_W)
    in_col = (qc[None, :] >= cs[:, None]) & (qc[None, :] < cs[:, None] + WIN_W)
    tiles = jnp.where(in_col, tiles, NEG_BIG)
    seq = tiles.transpose(0, 2, 1, 3).reshape(h, GRID_W, n_dr * GRID_W)
    variants = []
    for j in (0, 1, n_rows // NA_QROWS - 1):
        kb = int(np.clip(NA_QROWS * j - WIN_H // 2, 0, n_rows - NA_KROWS))
        strips = []
        for i in range(NA_QROWS):
            qr = NA_QROWS * j + i
            rs = int(np.clip(qr - WIN_H // 2, 0, n_rows - WIN_H))
            lead = rs - kb
            a0 = rs - qr + WIN_H - 1
            tail = NA_KROWS - WIN_H - lead
            strips.append(jnp.concatenate(
                [jnp.full((h, GRID_W, lead * GRID_W), NEG_BIG, F32),
                 seq[:, :, a0 * GRID_W:(a0 + WIN_H) * GRID_W],
                 jnp.full((h, GRID_W, tail * GRID_W), NEG_BIG, F32)], axis=-1))
        variants.append(jnp.concatenate(strips, axis=1))
    return jnp.stack(variants, axis=1)


def _na_attention(q, k, v, k_ctx, v_ctx, bias):
    b, n, _ = q.shape
    nj = n // NA_TQ
    n_kblk = n // NA_KBLK
    n_ctx = k_ctx.shape[1]

    def kstart(j):
        return jnp.clip(2 * j - 1, 0, n_kblk - 4)

    def kspec(i):
        return pl.BlockSpec((1, NA_KBLK, LANES), lambda p, j, bb: (bb, kstart(j) + i, p))

    def variant(j):
        return jnp.where(j == 0, 0, jnp.where(j == nj - 1, 2, 1))

    return pl.pallas_call(
        _na_kernel,
        grid=(2, nj, b),
        in_specs=[pl.BlockSpec((1, NA_TQ, LANES), lambda p, j, bb: (bb, j, p))]
        + [kspec(i) for i in range(4)] + [kspec(i) for i in range(4)]
        + [pl.BlockSpec((1, n_ctx, LANES), lambda p, j, bb: (bb, 0, p)),
           pl.BlockSpec((1, n_ctx, LANES), lambda p, j, bb: (bb, 0, p)),
           pl.BlockSpec((2, 1, NA_TQ, NA_KROWS * GRID_W), lambda p, j, bb: (p, variant(j), 0, 0))],
        out_specs=pl.BlockSpec((1, NA_TQ, LANES), lambda p, j, bb: (bb, j, p)),
        out_shape=jax.ShapeDtypeStruct((b, n, 256), BF16),
        compiler_params=_params(3),
        name="na_lat",
    )(q, k, k, k, k, v, v, v, v, k_ctx, v_ctx, bias)


LRU_SEG = 8
LRU_PAD = 8
LRU_ROWS = 512
LRU_HALVES = 2


def _put(ref, rows, val):
    for hv in range(LRU_HALVES):
        ref[hv, rows, :] = val[:, LANES * hv:LANES * (hv + 1)]


def _get(ref, rows):
    return jnp.concatenate([ref[hv, rows, :] for hv in range(LRU_HALVES)], axis=-1)


def _lru_gates(xp, n, cw, cb, wa_ref, ba_ref, wx_ref, bx_ref, sp, a_refs, u_refs):
    r = min(LRU_ROWS, n)
    for c0 in range(0, n, r):
        conv = cb
        for tap in range(CONV_W):
            conv = conv + cw[tap:tap + 1, :] * xp[pl.ds(LRU_PAD + c0 + tap - CONV_W // 2, r), :]
        xb = conv.astype(BF16)
        for d in range(2):
            rg = jax.nn.sigmoid(_dot(xb, wa_ref[d]) + ba_ref[d:d + 1, :])
            ig = jax.nn.sigmoid(_dot(xb, wx_ref[d]) + bx_ref[d:d + 1, :])
            log_a = (-LRU_C) * rg * sp[d:d + 1, :]
            _put(a_refs[d], pl.ds(c0, r), jnp.exp(log_a))
            _put(u_refs[d], pl.ds(c0, r), jnp.sqrt(1.0 - jnp.exp(2.0 * log_a)) * (ig * conv))


def _lru_scan(n, af, uf, ab, ub, hf0, hb0):
    seg = n // LRU_SEG

    def body(i, carry):
        rf = pl.ds(i, LRU_SEG, stride=seg)
        rb = pl.ds(seg - 1 - i, LRU_SEG, stride=seg)
        out = []
        for hv in range(LRU_HALVES):
            hlf, cmf, hlb, cmb = carry[4 * hv:4 * hv + 4]
            a1, u1 = af[hv, rf, :], uf[hv, rf, :]
            a2, u2 = ab[hv, rb, :], ub[hv, rb, :]
            hlf = a1 * hlf + u1
            cmf = a1 * cmf
            hlb = a2 * hlb + u2
            cmb = a2 * cmb
            uf[hv, rf, :] = hlf
            af[hv, rf, :] = cmf
            ub[hv, rb, :] = hlb
            ab[hv, rb, :] = cmb
            out += [hlf, cmf, hlb, cmb]
        return tuple(out)

    zero = jnp.zeros((LRU_SEG, LANES), F32)
    one = jnp.ones((LRU_SEG, LANES), F32)
    lax.fori_loop(0, seg, body, (zero, one, zero, one) * LRU_HALVES)
    hf, hb = list(hf0), list(hb0)
    for s in range(LRU_SEG):
        rf = pl.ds(s * seg, seg)
        rb = pl.ds((LRU_SEG - 1 - s) * seg, seg)
        for hv in range(LRU_HALVES):
            h = uf[hv, rf, :] + af[hv, rf, :] * hf[hv]
            uf[hv, rf, :] = h
            hf[hv] = h[seg - 1:seg, :]
            h = ub[hv, rb, :] + ab[hv, rb, :] * hb[hv]
            ub[hv, rb, :] = h
            hb[hv] = h[0:1, :]
    return hf, hb


def _lru_kernel(xl_ref, xc_ref, cw_ref, cb_ref, wa_ref, ba_ref, wx_ref, bx_ref, lam_ref,
                yl_ref, yc_ref, xp, afl, ufl, abl, ubl, afc, ufc, abc, ubc):
    n_lat, w = xl_ref.shape[1], xl_ref.shape[2]
    n_ctx = xc_ref.shape[1]
    cw, cb = cw_ref[...], cb_ref[...]
    neg_lam = -lam_ref[...]
    sp = jnp.maximum(neg_lam, 0.0) + jnp.log1p(jnp.exp(-jnp.abs(neg_lam)))
    zpad = jnp.zeros((LRU_PAD, w), F32)

    xp[pl.ds(0, LRU_PAD), :] = zpad
    xp[pl.ds(LRU_PAD, n_ctx), :] = xc_ref[0]
    xp[pl.ds(LRU_PAD + n_ctx, LRU_PAD), :] = zpad
    _lru_gates(xp, n_ctx, cw, cb, wa_ref, ba_ref, wx_ref, bx_ref, sp, (afc, abc), (ufc, ubc))
    xp[pl.ds(LRU_PAD, n_lat), :] = xl_ref[0]
    xp[pl.ds(LRU_PAD + n_lat, LRU_PAD), :] = zpad
    _lru_gates(xp, n_lat, cw, cb, wa_ref, ba_ref, wx_ref, bx_ref, sp, (afl, abl), (ufl, ubl))

    h0 = [jnp.zeros((1, LANES), F32)] * LRU_HALVES
    hf, hb = _lru_scan(n_ctx, afc, ufc, abc, ubc, h0, h0)
    _lru_scan(n_lat, afl, ufl, abl, ubl, hf, hb)
    rows = pl.ds(0, n_ctx)
    yc_ref[0] = (_get(ufc, rows) + _get(ubc, rows)).astype(BF16)
    for c0 in range(0, n_lat, LRU_ROWS):
        rows = pl.ds(c0, LRU_ROWS)
        yl_ref[0, rows, :] = (_get(ufl, rows) + _get(ubl, rows)).astype(BF16)


def _lru(x_lat, x_ctx, conv_w, conv_b, wa, ba, wx, bx, lam):
    b, n_lat, w = x_lat.shape
    n_ctx = x_ctx.shape[1]
    consts = (conv_w, conv_b, wa, ba, wx, bx, lam)
    big = lambda: pltpu.VMEM((LRU_HALVES, n_lat, LANES), F32)
    small = lambda: pltpu.VMEM((LRU_HALVES, n_ctx, LANES), F32)
    return pl.pallas_call(
        _lru_kernel,
        grid=(b,),
        in_specs=[pl.BlockSpec((1, n_lat, w), lambda bb: (bb, 0, 0)),
                  pl.BlockSpec((1, n_ctx, w), lambda bb: (bb, 0, 0))]
        + [_const_spec(c.shape) for c in consts],
        out_specs=[pl.BlockSpec((1, n_lat, w), lambda bb: (bb, 0, 0)),
                   pl.BlockSpec((1, n_ctx, w), lambda bb: (bb, 0, 0))],
        out_shape=[jax.ShapeDtypeStruct((b, n_lat, w), BF16),
                   jax.ShapeDtypeStruct((b, n_ctx, w), BF16)],
        scratch_shapes=[pltpu.VMEM((n_lat + 2 * LRU_PAD, w), F32),
                        big(), big(), big(), big(), small(), small(), small(), small()],
        compiler_params=_params(1),
        name="rglru",
    )(x_lat, x_ctx, *consts)


def _merge_kernel(x_ref, mod_ref, ya_ref, yb_ref, yc_ref, yd_ref, wz_ref, wm_ref, wb_ref, wo_ref,
                  lng_ref, lnb_ref, o_ref):
    x = x_ref[0]
    shift = mod_ref[0, 0:1, :]
    scale = mod_ref[0, 1:2, :]
    gate = mod_ref[0, 2:3, :]
    xm = (_layer_norm(x) * (1.0 + scale) + shift).astype(BF16)
    z = _dot(xm, wz_ref[...])
    acc = None
    for i, y_ref in enumerate((ya_ref, yb_ref, yc_ref, yd_ref)):
        zi = z[:, BRANCH_W * i:BRANCH_W * (i + 1)]
        g = (y_ref[0].astype(F32) * (zi * jax.nn.sigmoid(zi))).astype(BF16)
        t = _dot(g, wb_ref[i])
        mi = _dot(xm, wm_ref[:, D_MODEL * i:D_MODEL * (i + 1)])
        term = jax.nn.sigmoid(mi) * t
        acc = term if acc is None else acc + term
    out = _dot(acc.astype(BF16), wo_ref[...])
    o_ref[0] = _layer_norm(DEEPNORM_ALPHA * x + gate * out) * lng_ref[...] + lnb_ref[...]


def _merge(x, mod, ys, wz, wm, wb, wo, ln_g, ln_b, tile):
    b, n, _ = x.shape
    consts = (wz, wm, wb, wo, ln_g, ln_b)
    tok = lambda w: pl.BlockSpec((1, tile, w), lambda bb, t: (bb, t, 0))
    return pl.pallas_call(
        _merge_kernel,
        grid=(b, n // tile),
        in_specs=[tok(D_MODEL), pl.BlockSpec((1, 3, D_MODEL), lambda bb, t: (bb, 0, 0))]
        + [tok(BRANCH_W) for _ in ys] + [_const_spec(c.shape) for c in consts],
        out_specs=tok(D_MODEL),
        out_shape=jax.ShapeDtypeStruct((b, n, D_MODEL), F32),
        compiler_params=_params(2),
        name="merge",
    )(x, mod, *ys, *consts)


def _rope_tables(n):
    t = jnp.arange(n, dtype=jnp.int32)
    rows = (t // GRID_W).astype(F32)[:, None]
    cols = (t % GRID_W).astype(F32)[:, None]

    def table(half, offset):
        inv = ROPE_THETA ** (-jnp.arange(0, 2 * half, 2, dtype=F32) / (2 * half))
        cos = jnp.ones((n, LANES), F32)
        sin = jnp.zeros((n, LANES), F32)
        for g, pos in enumerate((rows, cols)):
            ang = pos * inv[None, :]
            c, s = jnp.cos(ang), jnp.sin(ang)
            base = offset + 2 * half * g
            cos = cos.at[:, base:base + 2 * half].set(jnp.concatenate([c, c], -1))
            sin = sin.at[:, base:base + 2 * half].set(jnp.concatenate([-s, s], -1))
        return cos, sin

    cd, sd = table(16, 0)
    cosd = jnp.concatenate([cd[:, :64], cd[:, :64]], -1)
    sind = jnp.concatenate([sd[:, :64], sd[:, :64]], -1)
    cosb, sinb = table(8, MLA_NOPE)
    cosk, sink = table(8, 0)
    return cosd, sind, cosb, sinb, cosk, sink


def _identity_tables(n):
    one, zero = jnp.ones((n, LANES), F32), jnp.zeros((n, LANES), F32)
    return one, zero, one, zero, one, zero


def _layer_weights(w_in, mla_q_norm, mla_w_uq, mla_kv_norm, mla_w_ukv, gqa_q_norm, gqa_k_norm, w_branch, w_out):
    wi = w_in[:, :MIX_COLS]
    qa, ka, va = wi[:, 0:256], wi[:, 256:512], wi[:, 512:768]
    cq, ckv, kr = wi[:, 768:1024], wi[:, 1024:1152], wi[:, 1152:1184]
    lru, qd, kd, vd = wi[:, 1184:1440], wi[:, 1440:1696], wi[:, 1696:1824], wi[:, 1824:1952]
    head_order = jnp.array([0, 2, 1, 3])
    qd = qd.reshape(D_MODEL, 4, HEAD_DIM)[:, head_order].reshape(D_MODEL, 256)
    wmix = jnp.concatenate([qa, ka, va, cq, ckv, lru, qd, kd, vd, kr,
                            jnp.zeros((D_MODEL, MIX_PAD - MIX_COLS), F32)], -1).astype(BF16)
    hq = MLA_NOPE + MLA_ROPE
    wuq = jnp.pad(mla_w_uq.reshape(256, MLA_HEADS, hq), ((0, 0), (0, 0), (0, LANES - hq)))
    wuq = wuq.reshape(256, MLA_HEADS * LANES).astype(BF16)
    ukv = mla_w_ukv.reshape(128, MLA_HEADS, 128)
    wuk = jnp.pad(ukv[:, :, :MLA_NOPE], ((0, 0), (0, 0), (0, LANES - MLA_NOPE)))
    wuk = wuk.reshape(128, MLA_HEADS * LANES).astype(BF16)
    wuv = ukv[:, :, MLA_NOPE:].reshape(128, 256).astype(BF16)
    pk = np.zeros((LANES, MLA_HEADS * LANES), np.float32)
    for h in range(MLA_HEADS):
        pk[np.arange(MLA_ROPE), LANES * h + MLA_NOPE + np.arange(MLA_ROPE)] = 1.0
    grp = np.kron(np.eye(4, dtype=np.float32), np.ones((HEAD_DIM, HEAD_DIM), np.float32))
    inproj_w = (wmix, wuq, wuk, wuv, jnp.asarray(pk, BF16), jnp.asarray(grp, BF16),
                mla_q_norm.reshape(1, 256), mla_kv_norm.reshape(1, 128),
                jnp.tile(gqa_q_norm, 4).reshape(1, 256), jnp.tile(gqa_k_norm, 2).reshape(1, 128))
    wz = w_in[:, MIX_COLS:MIX_COLS + SILU_COLS]
    wz_d = wz[:, 3 * BRANCH_W:].reshape(D_MODEL, 4, HEAD_DIM)[:, head_order].reshape(D_MODEL, BRANCH_W)
    wz = jnp.concatenate([wz[:, :3 * BRANCH_W], wz_d], -1).astype(BF16)
    wm = w_in[:, MIX_COLS + SILU_COLS:].astype(BF16)
    wb_d = w_branch[3].reshape(4, HEAD_DIM, D_MODEL)[head_order].reshape(BRANCH_W, D_MODEL)
    wb = jnp.concatenate([w_branch[:3], wb_d[None]], 0).astype(BF16)
    return inproj_w, (wz, wm, wb, w_out.astype(BF16))


def _block_diag(w):
    eye = jnp.eye(4, dtype=w.dtype)
    return jnp.einsum("dkce,kj->dkcje", w, eye).reshape(2, 256, 256)


def kernel(x, c, ctx, c_ctx, w_mod, b_mod, w_in, na_rel_bias, mla_q_norm, mla_w_uq, mla_kv_norm, mla_w_ukv,
           lru_conv_w, lru_conv_b, lru_w_a, lru_b_a, lru_w_x, lru_b_x, lru_lambda, gqa_q_norm, gqa_k_norm,
           w_branch, w_out, ln_g, ln_b):
    b, n, _ = x.shape
    n_ctx = ctx.shape[1]
    rows = 8 * ((b + 1 + 7) // 8)
    c_all = jnp.zeros((rows, D_MODEL), F32).at[:b].set(c).at[b].set(c_ctx)
    mod_all = _modulation(c_all, w_mod, b_mod)
    lat_tables = _rope_tables(n)
    ctx_tables = _identity_tables(n_ctx)

    for l in range(DEPTH):
        need_ctx = l < DEPTH - 1
        mod_lat = mod_all[l, :b].reshape(b, 3, D_MODEL)
        mod_ctx = jnp.broadcast_to(mod_all[l, b].reshape(1, 3, D_MODEL), (b, 3, D_MODEL))
        inproj_w, merge_w = _layer_weights(w_in[l], mla_q_norm[l], mla_w_uq[l], mla_kv_norm[l], mla_w_ukv[l],
                                           gqa_q_norm[l], gqa_k_norm[l], w_branch[l], w_out[l])
        qa, ka, va, qb, kb, vb, xr, qd, kd, vd = _inproj(x, mod_lat, inproj_w, lat_tables, 512)
        qa_c, ka_c, va_c, qb_c, kb_c, vb_c, xr_c, qd_c, kd_c, vd_c = _inproj(
            ctx, mod_ctx, inproj_w, ctx_tables, n_ctx)

        ya = _na_attention(qa, ka, va, ka_c, va_c, _na_bias(na_rel_bias[l]))
        yb = _attention(qb, kb, vb, kb_c, vb_c, mode="B", tq=512)
        yd = _attention(qd, kd, vd, kd_c, vd_c, mode="D", tq=512)
        yc, yc_c = _lru(xr, xr_c, lru_conv_w[l], lru_conv_b[l].reshape(1, -1),
                        _block_diag(lru_w_a[l]).astype(BF16), lru_b_a[l],
                        _block_diag(lru_w_x[l]).astype(BF16), lru_b_x[l], lru_lambda[l])
        tail = (*merge_w, ln_g[l].reshape(1, -1), ln_b[l].reshape(1, -1))
        x_new = _merge(x, mod_lat, (ya, yb, yc, yd), *tail, 512)
        if need_ctx:
            ya_c = _attention(qa_c, None, None, ka_c, va_c, mode="A", tq=n_ctx)
            yb_c = _attention(qb_c, None, None, kb_c, vb_c, mode="B", tq=n_ctx)
            yd_c = _attention(qd_c, None, None, kd_c, vd_c, mode="D", tq=n_ctx)
            ctx = _merge(ctx, mod_ctx, (ya_c, yb_c, yc_c, yd_c), *tail, n_ctx)
        x = x_new
    return x
```

```python
import functools
import math

import numpy as np
import jax
import jax.numpy as jnp
from jax import lax
from jax.experimental import pallas as pl
from jax.experimental.pallas import tpu as pltpu

F32 = jnp.float32
BF16 = jnp.bfloat16

D_MODEL = 1024
DEPTH = 2
GRID_W = 64
BRANCH_W = 256
HEAD_DIM = 64
WIN_H = 8
WIN_W = 16
MLA_HEADS = 4
MLA_NOPE = 64
MLA_ROPE = 32
LRU_C = 8.0
CONV_W = 4
ROPE_THETA = 10000.0
EPS = 1e-6
LOG2E = 1.4426950408889634
NA_SCALE = HEAD_DIM ** -0.5
MLA_SCALE = (MLA_NOPE + MLA_ROPE) ** -0.5
GQA_SCALE = HEAD_DIM ** -0.5
DEEPNORM_ALPHA = (2 * DEPTH) ** 0.25

MIX_COLS = 1952
SILU_COLS = 1024
LANES = 128
NEG_BIG = -1e30
VMEM_LIMIT = 56 * 1024 * 1024

OFF_QA, OFF_KA, OFF_VA, OFF_CQ, OFF_CKV, OFF_LRU, OFF_QD, OFF_KD, OFF_VD, OFF_KR = (
    0, 256, 512, 768, 1024, 1152, 1408, 1664, 1792, 1920)
MIX_PAD = 2048


def _dot(a, b):
    return jnp.dot(a, b, preferred_element_type=F32)


def _dot_t(a, b):
    return lax.dot_general(a, b, (((1,), (1,)), ((), ())), preferred_element_type=F32)


def _layer_norm(x):
    mu = jnp.mean(x, axis=-1, keepdims=True)
    xc = x - mu
    var = jnp.mean(xc * xc, axis=-1, keepdims=True)
    return xc * lax.rsqrt(var + EPS)


def _sigmoid(x):
    return 0.5 * jnp.tanh(0.5 * x) + 0.5


def _rms(x, g):
    return x * lax.rsqrt(jnp.mean(x * x, axis=-1, keepdims=True) + EPS) * g


def _params(n_grid):
    return pltpu.CompilerParams(dimension_semantics=("arbitrary",) * n_grid,
                                vmem_limit_bytes=VMEM_LIMIT)


def _const_spec(shape):
    zeros = (0,) * len(shape)
    return pl.BlockSpec(shape, lambda *_: zeros)


def _mod_kernel(c_ref, w_ref, b_ref, o_ref):
    c = c_ref[...]
    s = (c * jax.nn.sigmoid(c)).astype(BF16)
    o_ref[0] = _dot(s, w_ref[0].astype(BF16)) + b_ref[0]


def _modulation(c_all, w_mod, b_mod):
    n_l, _, n_out = w_mod.shape
    rows = c_all.shape[0]
    bn = 512
    return pl.pallas_call(
        _mod_kernel,
        grid=(n_l, n_out // bn),
        in_specs=[pl.BlockSpec((rows, D_MODEL), lambda l, n: (0, 0)),
                  pl.BlockSpec((1, D_MODEL, bn), lambda l, n: (l, 0, n)),
                  pl.BlockSpec((1, 1, bn), lambda l, n: (l, 0, n))],
        out_specs=pl.BlockSpec((1, rows, bn), lambda l, n: (l, 0, n)),
        out_shape=jax.ShapeDtypeStruct((n_l, rows, n_out), F32),
        compiler_params=_params(2),
        name="modulation",
    )(c_all, w_mod, b_mod.reshape(n_l, 1, n_out))


def _pair_select(lane, shift):
    lane_up = pltpu.roll(lane, LANES - shift, 1)
    want = jnp.where((lane & (2 * shift - 1)) < shift, lane + shift, lane - shift)
    return lane_up == want


def _rope(x, cos, sin, shift, sel):
    partner = jnp.where(sel, pltpu.roll(x, LANES - shift, 1), pltpu.roll(x, shift, 1))
    return x * cos + partner * sin


def _group_sumsq(x, g):
    sq = x * x
    hi = sq.astype(BF16)
    lo = (sq - hi.astype(F32)).astype(BF16)
    return _dot(hi, g) + _dot(lo, g)


def _inproj_kernel(x_ref, mod_ref, wmix_ref, wuq_ref, wuk_ref, wuv_ref, pk_ref, g_ref,
                   gq_ref, gkv_ref, gdq_ref, gdk_ref,
                   cosd_ref, sind_ref, cosb_ref, sinb_ref, cosk_ref, sink_ref,
                   qa_o, ka_o, va_o, qb_o, kb_o, vb_o, xc_o, qd_o, kd_o, vd_o, vat_o, vbt_o, vdt_o):
    x = x_ref[0]
    shift = mod_ref[0, 0:1, :]
    scale = mod_ref[0, 1:2, :]
    xm = (_layer_norm(x) * (1.0 + scale) + shift).astype(BF16)
    p = _dot(xm, wmix_ref[...])
    t = x.shape[0]
    lane = lax.broadcasted_iota(jnp.int32, (t, LANES), 1)
    sel16 = _pair_select(lane, 16)
    sel8 = _pair_select(lane, 8)

    qa_o[0] = (p[:, OFF_QA:OFF_QA + 256] * (NA_SCALE * LOG2E)).astype(BF16)
    ka_o[0] = p[:, OFF_KA:OFF_KA + 256].astype(BF16)
    va = p[:, OFF_VA:OFF_VA + 256]
    va_o[0] = va.astype(BF16)
    vat_o[0] = va.T.astype(BF16)

    cqn = _rms(p[:, OFF_CQ:OFF_CQ + 256], gq_ref[...]).astype(BF16)
    qb = _dot(cqn, wuq_ref[...])
    cosb, sinb = cosb_ref[...], sinb_ref[...]
    for h in range(MLA_HEADS):
        blk = _rope(qb[:, LANES * h:LANES * (h + 1)], cosb, sinb, 8, sel8)
        qb_o[0, :, LANES * h:LANES * (h + 1)] = (blk * (MLA_SCALE * LOG2E)).astype(BF16)
    ckvn = _rms(p[:, OFF_CKV:OFF_CKV + 128], gkv_ref[...]).astype(BF16)
    kr = _rope(p[:, OFF_KR:OFF_KR + 128], cosk_ref[...], sink_ref[...], 8, sel8).astype(BF16)
    kb_o[0] = (_dot(ckvn, wuk_ref[...]) + _dot(kr, pk_ref[...])).astype(BF16)
    vb = _dot(ckvn, wuv_ref[...])
    vb_o[0] = vb.astype(BF16)
    vbt_o[0] = vb.T.astype(BF16)

    xc_o[0] = p[:, OFF_LRU:OFF_LRU + 256]

    cosd, sind = cosd_ref[...], sind_ref[...]
    qd = p[:, OFF_QD:OFF_QD + 256]
    qd = qd * lax.rsqrt(_group_sumsq(qd, g_ref[...]) * (1.0 / HEAD_DIM) + EPS) * gdq_ref[...]
    for blk_i in range(2):
        blk = _rope(qd[:, LANES * blk_i:LANES * (blk_i + 1)], cosd, sind, 16, sel16)
        qd_o[0, :, LANES * blk_i:LANES * (blk_i + 1)] = (blk * (GQA_SCALE * LOG2E)).astype(BF16)
    kd = p[:, OFF_KD:OFF_KD + 128]
    kd = kd * lax.rsqrt(_group_sumsq(kd, g_ref[0:128, 0:128]) * (1.0 / HEAD_DIM) + EPS) * gdk_ref[...]
    kd_o[0] = _rope(kd, cosd, sind, 16, sel16).astype(BF16)
    vd = p[:, OFF_VD:OFF_VD + 128]
    vd_o[0] = vd.astype(BF16)
    vdt_o[0] = vd.T.astype(BF16)


def _inproj(x, mod, weights, tables, tile):
    b, n, _ = x.shape
    nt = n // tile
    wspecs = [_const_spec(w.shape) for w in weights]
    tspecs = [pl.BlockSpec((tile, LANES), lambda t, bb: (t, 0)) for _ in tables]
    widths = (256, 256, 256, 512, 512, 256, 256, 256, 128, 128)
    dtypes = (BF16, BF16, BF16, BF16, BF16, BF16, F32, BF16, BF16, BF16)
    t_rows = (256, 256, 128)
    return pl.pallas_call(
        _inproj_kernel,
        grid=(nt, b),
        in_specs=[pl.BlockSpec((1, tile, D_MODEL), lambda t, bb: (bb, t, 0)),
                  pl.BlockSpec((1, 3, D_MODEL), lambda t, bb: (bb, 0, 0))] + wspecs + tspecs,
        out_specs=[pl.BlockSpec((1, tile, w), lambda t, bb: (bb, t, 0)) for w in widths]
        + [pl.BlockSpec((1, r, tile), lambda t, bb: (bb, 0, t)) for r in t_rows],
        out_shape=[jax.ShapeDtypeStruct((b, n, w), dt) for w, dt in zip(widths, dtypes)]
        + [jax.ShapeDtypeStruct((b, r, n), BF16) for r in t_rows],
        compiler_params=_params(2),
        name="inproj",
    )(x, mod, *weights, *tables)


def _half_mask(lane, hh):
    return lane >= HEAD_DIM if hh else lane < HEAD_DIM


def _chunk_update(qh, kc, vc, m, l, acc):
    s = _dot_t(qh, kc)
    m_new = jnp.maximum(m, jnp.max(s, axis=-1, keepdims=True))
    alpha = jnp.exp2(m - m_new)
    pr = jnp.exp2(s - m_new)
    l = alpha * l + jnp.sum(pr, axis=-1, keepdims=True)
    acc = alpha * acc + _dot(pr.astype(BF16), vc)
    return m_new, l, acc


def _attn_kernel(*refs, masked, n_lat, chunk):
    if n_lat:
        q_ref, kl_ref, vl_ref, kc_ref, vc_ref, o_ref = refs
    else:
        q_ref, kc_ref, vc_ref, o_ref = refs
    tq = q_ref.shape[1]
    lane = lax.broadcasted_iota(jnp.int32, (tq, LANES), 1)
    qs, ksls = [], []
    for hh in range(2):
        if masked:
            qs.append(jnp.where(_half_mask(lane, hh), q_ref[0], jnp.zeros((), BF16)))
            ksls.append(slice(None))
        else:
            qs.append(q_ref[0, :, LANES * hh:LANES * (hh + 1)])
            ksls.append(slice(LANES * hh, LANES * (hh + 1)))
    carries = [(jnp.full((tq, 1), NEG_BIG, F32), jnp.zeros((tq, 1), F32), jnp.zeros((tq, LANES), F32))
               for _ in range(2)]
    for c0 in range(0, n_lat, chunk):
        rows = pl.ds(c0, chunk)
        for hh in range(2):
            carries[hh] = _chunk_update(qs[hh], kl_ref[0, rows, ksls[hh]], vl_ref[0, rows, :], *carries[hh])
    outs = []
    for hh in range(2):
        m, l, acc = _chunk_update(qs[hh], kc_ref[0, :, ksls[hh]], vc_ref[0], *carries[hh])
        outs.append(acc * (1.0 / l))
    o_ref[0] = jnp.where(lane < HEAD_DIM, outs[0], outs[1]).astype(BF16)


def _attention(q, k_lat, v_lat, k_ctx, v_ctx, *, mode, tq, chunk=1024):
    b, nq, _ = q.shape
    masked = mode != "B"
    qw = 256 if mode == "B" else LANES
    kw = 256 if mode == "B" else LANES
    kidx = (lambda bb, p, t: (bb, 0, 0)) if mode == "D" else (lambda bb, p, t: (bb, 0, p))
    n_ctx = k_ctx.shape[1]
    n_lat = 0 if k_lat is None else k_lat.shape[1]
    in_specs = [pl.BlockSpec((1, tq, qw), lambda bb, p, t: (bb, t, p))]
    args = [q]
    if n_lat:
        in_specs += [pl.BlockSpec((1, n_lat, kw), kidx), pl.BlockSpec((1, n_lat, LANES), kidx)]
        args += [k_lat, v_lat]
    in_specs += [pl.BlockSpec((1, n_ctx, kw), kidx), pl.BlockSpec((1, n_ctx, LANES), kidx)]
    args += [k_ctx, v_ctx]
    return pl.pallas_call(
        functools.partial(_attn_kernel, masked=masked, n_lat=n_lat, chunk=chunk),
        grid=(b, 2, nq // tq),
        in_specs=in_specs,
        out_specs=pl.BlockSpec((1, tq, LANES), lambda bb, p, t: (bb, t, p)),
        out_shape=jax.ShapeDtypeStruct((b, nq, 256), BF16),
        compiler_params=_params(3),
        name="attn_" + mode + ("_lat" if n_lat else "_ctx"),
    )(*args)


def _softmax_update_t(s, vtc, m, l, acc):
    m_new = jnp.maximum(m, jnp.max(s, axis=0, keepdims=True))
    alpha = jnp.exp2(m - m_new)
    pr = jnp.exp2(s - m_new)
    l = alpha * l + jnp.sum(pr, axis=0, keepdims=True)
    acc = alpha * acc + _dot(vtc, pr.astype(BF16))
    return m_new, l, acc


def _attn_t_kernel(q_ref, kl_ref, vtl_ref, kc_ref, vtc_ref, o_ref, *, masked, chunk):
    tq = q_ref.shape[1]
    n_lat = kl_ref.shape[1]
    lane = lax.broadcasted_iota(jnp.int32, (tq, LANES), 1)
    qs, ksls = [], []
    for hh in range(2):
        if masked:
            qs.append(jnp.where(_half_mask(lane, hh), q_ref[0], jnp.zeros((), BF16)))
            ksls.append(slice(None))
        else:
            qs.append(q_ref[0, :, LANES * hh:LANES * (hh + 1)])
            ksls.append(slice(LANES * hh, LANES * (hh + 1)))
    carries = [(jnp.full((1, tq), NEG_BIG, F32), jnp.zeros((1, tq), F32), jnp.zeros((LANES, tq), F32))
               for _ in range(2)]
    tasks = [(hh, (kl_ref, pl.ds(c0, chunk)), (vtl_ref, pl.ds(c0, chunk)))
             for c0 in range(0, n_lat, chunk) for hh in range(2)]
    tasks += [(hh, (kc_ref, slice(None)), (vtc_ref, slice(None))) for hh in range(2)]

    def scores(task):
        hh, (k_ref, rows), _ = task
        return _dot_t(k_ref[0, rows, ksls[hh]], qs[hh])

    s_next = scores(tasks[0])
    for i, task in enumerate(tasks):
        s_cur = s_next
        if i + 1 < len(tasks):
            s_next = scores(tasks[i + 1])
        hh, _, (vt_ref, cols) = task
        carries[hh] = _softmax_update_t(s_cur, vt_ref[0, :, cols], *carries[hh])
    row = lax.broadcasted_iota(jnp.int32, (LANES, tq), 0)
    outs = [acc * (1.0 / l) for _, l, acc in carries]
    o_ref[0] = jnp.where(row < HEAD_DIM, outs[0], outs[1]).T.astype(BF16)


def _attention_t(q, k_lat, vt_lat, k_ctx, vt_ctx, *, mode, tq, chunk=1024):
    b, nq, _ = q.shape
    masked = mode != "B"
    qw = 256 if mode == "B" else LANES
    kidx = (lambda bb, p, t: (bb, 0, 0)) if mode == "D" else (lambda bb, p, t: (bb, 0, p))
    vidx = (lambda bb, p, t: (bb, 0, 0)) if mode == "D" else (lambda bb, p, t: (bb, p, 0))
    n_lat, n_ctx = k_lat.shape[1], k_ctx.shape[1]
    return pl.pallas_call(
        functools.partial(_attn_t_kernel, masked=masked, chunk=chunk),
        grid=(b, 2, nq // tq),
        in_specs=[pl.BlockSpec((1, tq, qw), lambda bb, p, t: (bb, t, p)),
                  pl.BlockSpec((1, n_lat, qw), kidx), pl.BlockSpec((1, LANES, n_lat), vidx),
                  pl.BlockSpec((1, n_ctx, qw), kidx), pl.BlockSpec((1, LANES, n_ctx), vidx)],
        out_specs=pl.BlockSpec((1, tq, LANES), lambda bb, p, t: (bb, t, p)),
        out_shape=jax.ShapeDtypeStruct((b, nq, 256), BF16),
        compiler_params=_params(3),
        name="attn_" + mode + "_lat",
    )(q, k_lat, vt_lat, k_ctx, vt_ctx)


NA_QROWS = 8
NA_KROWS = 16
NA_TQ = NA_QROWS * GRID_W
NA_KBLK = 256


def _na_kernel(q_ref, k0, k1, k2, k3, vt0, vt1, vt2, vt3, kc_ref, vtc_ref, bias_ref, o_ref):
    q = q_ref[0]
    lane = lax.broadcasted_iota(jnp.int32, (NA_TQ, LANES), 1)
    k_win = jnp.concatenate([k0[0], k1[0], k2[0], k3[0]], axis=0)
    vt_win = jnp.concatenate([vt0[0], vt1[0], vt2[0], vt3[0]], axis=1)
    qs = [jnp.where(_half_mask(lane, hh), q, jnp.zeros((), BF16)) for hh in range(2)]
    scores = [(_dot_t(k_win, qh) + bias_ref[hh, 0], _dot_t(kc_ref[0], qh)) for hh, qh in enumerate(qs)]
    outs = []
    for s_win, s_ctx in scores:
        m = jnp.maximum(jnp.max(s_win, axis=0, keepdims=True), jnp.max(s_ctx, axis=0, keepdims=True))
        p_win = jnp.exp2(s_win - m)
        p_ctx = jnp.exp2(s_ctx - m)
        l = jnp.sum(p_win, axis=0, keepdims=True) + jnp.sum(p_ctx, axis=0, keepdims=True)
        acc = _dot(vt_win, p_win.astype(BF16)) + _dot(vtc_ref[0], p_ctx.astype(BF16))
        outs.append(acc * (1.0 / l))
    row = lax.broadcasted_iota(jnp.int32, (LANES, NA_TQ), 0)
    o_ref[0] = jnp.where(row < HEAD_DIM, outs[0], outs[1]).T.astype(BF16)


def _na_bias(rel_bias):
    h, n_dr, n_dc = rel_bias.shape
    n_rows = 64
    w2 = 2 * GRID_W
    pad_lo = GRID_W - WIN_W
    p = jnp.pad(rel_bias * LOG2E, ((0, 0), (0, 0), (pad_lo, w2 - pad_lo - n_dc)))
    skew = jnp.broadcast_to(p[:, :, None, :], (h, n_dr, GRID_W, w2)).reshape(h, n_dr, GRID_W * w2)
    skew = skew[:, :, :GRID_W * (w2 - 1)].reshape(h, n_dr, GRID_W, w2 - 1)
    tiles = skew[..., GRID_W - 1:]
    qc = np.arange(GRID_W)
    cs = np.clip(qc - WIN_W // 2, 0, GRID_W - WIN_W)
    in_col = (qc[None, :] >= cs[:, None]) & (qc[None, :] < cs[:, None] + WIN_W)
    tiles = jnp.where(in_col, tiles, NEG_BIG)
    seq = tiles.transpose(0, 2, 1, 3).reshape(h, GRID_W, n_dr * GRID_W)
    variants = []
    for j in (0, 1, n_rows // NA_QROWS - 1):
        kb = int(np.clip(NA_QROWS * j - WIN_H // 2, 0, n_rows - NA_KROWS))
        strips = []
        for i in range(NA_QROWS):
            qr = NA_QROWS * j + i
            rs = int(np.clip(qr - WIN_H // 2, 0, n_rows - WIN_H))
            lead = rs - kb
            a0 = rs - qr + WIN_H - 1
            tail = NA_KROWS - WIN_H - lead
            strips.append(jnp.concatenate(
                [jnp.full((h, GRID_W, lead * GRID_W), NEG_BIG, F32),
                 seq[:, :, a0 * GRID_W:(a0 + WIN_H) * GRID_W],
                 jnp.full((h, GRID_W, tail * GRID_W), NEG_BIG, F32)], axis=-1))
        variants.append(jnp.concatenate(strips, axis=1))
    return jnp.swapaxes(jnp.stack(variants, axis=1), -1, -2)


def _na_attention(q, k, vt, k_ctx, vt_ctx, bias):
    b, n, _ = q.shape
    nj = n // NA_TQ
    n_kblk = n // NA_KBLK
    n_ctx = k_ctx.shape[1]

    def kstart(j):
        return jnp.clip(2 * j - 1, 0, n_kblk - 4)

    def kspec(i):
        return pl.BlockSpec((1, NA_KBLK, LANES), lambda p, j, bb: (bb, kstart(j) + i, p))

    def vspec(i):
        return pl.BlockSpec((1, LANES, NA_KBLK), lambda p, j, bb: (bb, p, kstart(j) + i))

    def variant(j):
        return jnp.where(j == 0, 0, jnp.where(j == nj - 1, 2, 1))

    return pl.pallas_call(
        _na_kernel,
        grid=(2, nj, b),
        in_specs=[pl.BlockSpec((1, NA_TQ, LANES), lambda p, j, bb: (bb, j, p))]
        + [kspec(i) for i in range(4)] + [vspec(i) for i in range(4)]
        + [pl.BlockSpec((1, n_ctx, LANES), lambda p, j, bb: (bb, 0, p)),
           pl.BlockSpec((1, LANES, n_ctx), lambda p, j, bb: (bb, p, 0)),
           pl.BlockSpec((2, 1, NA_KROWS * GRID_W, NA_TQ), lambda p, j, bb: (p, variant(j), 0, 0))],
        out_specs=pl.BlockSpec((1, NA_TQ, LANES), lambda p, j, bb: (bb, j, p)),
        out_shape=jax.ShapeDtypeStruct((b, n, 256), BF16),
        compiler_params=_params(3),
        name="na_lat",
    )(q, k, k, k, k, vt, vt, vt, vt, k_ctx, vt_ctx, bias)


LRU_SEG = 8
LRU_PAD = 8
LRU_ROWS = 512
LRU_HALVES = 2


def _put(ref, rows, val):
    for hv in range(LRU_HALVES):
        ref[hv, rows, :] = val[:, LANES * hv:LANES * (hv + 1)]


def _get(ref, rows):
    return jnp.concatenate([ref[hv, rows, :] for hv in range(LRU_HALVES)], axis=-1)


def _lru_gates(xp, n, cw, cb, wa_ref, ba_ref, wx_ref, bx_ref, sp, a_refs, u_refs):
    r = min(LRU_ROWS, n)
    for c0 in range(0, n, r):
        conv = cb
        for tap in range(CONV_W):
            conv = conv + cw[tap:tap + 1, :] * xp[pl.ds(LRU_PAD + c0 + tap - CONV_W // 2, r), :]
        xb = conv.astype(BF16)
        for d in range(2):
            rg = _sigmoid(_dot(xb, wa_ref[d]) + ba_ref[d:d + 1, :])
            ig = _sigmoid(_dot(xb, wx_ref[d]) + bx_ref[d:d + 1, :])
            a = jnp.exp((-LRU_C) * rg * sp[d:d + 1, :])
            _put(a_refs[d], pl.ds(c0, r), a)
            _put(u_refs[d], pl.ds(c0, r), jnp.sqrt(1.0 - a * a) * (ig * conv))


def _lru_scan(n, af, uf, ab, ub, hf0, hb0):
    seg = n // LRU_SEG

    def body(i, carry):
        rf = pl.ds(i, LRU_SEG, stride=seg)
        rb = pl.ds(seg - 1 - i, LRU_SEG, stride=seg)
        out = []
        for hv in range(LRU_HALVES):
            hlf, cmf, hlb, cmb = carry[4 * hv:4 * hv + 4]
            a1, u1 = af[hv, rf, :], uf[hv, rf, :]
            a2, u2 = ab[hv, rb, :], ub[hv, rb, :]
            hlf = a1 * hlf + u1
            cmf = a1 * cmf
            hlb = a2 * hlb + u2
            cmb = a2 * cmb
            uf[hv, rf, :] = hlf
            af[hv, rf, :] = cmf
            ub[hv, rb, :] = hlb
            ab[hv, rb, :] = cmb
            out += [hlf, cmf, hlb, cmb]
        return tuple(out)

    zero = jnp.zeros((LRU_SEG, LANES), F32)
    one = jnp.ones((LRU_SEG, LANES), F32)
    lax.fori_loop(0, seg, body, (zero, one, zero, one) * LRU_HALVES)
    hf, hb = list(hf0), list(hb0)
    for s in range(LRU_SEG):
        rf = pl.ds(s * seg, seg)
        rb = pl.ds((LRU_SEG - 1 - s) * seg, seg)
        for hv in range(LRU_HALVES):
            h = uf[hv, rf, :] + af[hv, rf, :] * hf[hv]
            uf[hv, rf, :] = h
            hf[hv] = h[seg - 1:seg, :]
            h = ub[hv, rb, :] + ab[hv, rb, :] * hb[hv]
            ub[hv, rb, :] = h
            hb[hv] = h[0:1, :]
    return hf, hb


def _lru_kernel(xl_ref, xc_ref, cw_ref, cb_ref, wa_ref, ba_ref, wx_ref, bx_ref, lam_ref,
                yl_ref, yc_ref, xp, afl, ufl, abl, ubl, afc, ufc, abc, ubc):
    n_lat, w = xl_ref.shape[1], xl_ref.shape[2]
    n_ctx = xc_ref.shape[1]
    cw, cb = cw_ref[...], cb_ref[...]
    neg_lam = -lam_ref[...]
    sp = jnp.maximum(neg_lam, 0.0) + jnp.log1p(jnp.exp(-jnp.abs(neg_lam)))
    zpad = jnp.zeros((LRU_PAD, w), F32)

    xp[pl.ds(0, LRU_PAD), :] = zpad
    xp[pl.ds(LRU_PAD, n_ctx), :] = xc_ref[0]
    xp[pl.ds(LRU_PAD + n_ctx, LRU_PAD), :] = zpad
    _lru_gates(xp, n_ctx, cw, cb, wa_ref, ba_ref, wx_ref, bx_ref, sp, (afc, abc), (ufc, ubc))
    xp[pl.ds(LRU_PAD, n_lat), :] = xl_ref[0]
    xp[pl.ds(LRU_PAD + n_lat, LRU_PAD), :] = zpad
    _lru_gates(xp, n_lat, cw, cb, wa_ref, ba_ref, wx_ref, bx_ref, sp, (afl, abl), (ufl, ubl))

    h0 = [jnp.zeros((1, LANES), F32)] * LRU_HALVES
    hf, hb = _lru_scan(n_ctx, afc, ufc, abc, ubc, h0, h0)
    _lru_scan(n_lat, afl, ufl, abl, ubl, hf, hb)
    rows = pl.ds(0, n_ctx)
    yc_ref[0] = (_get(ufc, rows) + _get(ubc, rows)).astype(BF16)
    for c0 in range(0, n_lat, LRU_ROWS):
        rows = pl.ds(c0, LRU_ROWS)
        yl_ref[0, rows, :] = (_get(ufl, rows) + _get(ubl, rows)).astype(BF16)


def _lru(x_lat, x_ctx, conv_w, conv_b, wa, ba, wx, bx, lam):
    b, n_lat, w = x_lat.shape
    n_ctx = x_ctx.shape[1]
    consts = (conv_w, conv_b, wa, ba, wx, bx, lam)
    big = lambda: pltpu.VMEM((LRU_HALVES, n_lat, LANES), F32)
    small = lambda: pltpu.VMEM((LRU_HALVES, n_ctx, LANES), F32)
    return pl.pallas_call(
        _lru_kernel,
        grid=(b,),
        in_specs=[pl.BlockSpec((1, n_lat, w), lambda bb: (bb, 0, 0)),
                  pl.BlockSpec((1, n_ctx, w), lambda bb: (bb, 0, 0))]
        + [_const_spec(c.shape) for c in consts],
        out_specs=[pl.BlockSpec((1, n_lat, w), lambda bb: (bb, 0, 0)),
                   pl.BlockSpec((1, n_ctx, w), lambda bb: (bb, 0, 0))],
        out_shape=[jax.ShapeDtypeStruct((b, n_lat, w), BF16),
                   jax.ShapeDtypeStruct((b, n_ctx, w), BF16)],
        scratch_shapes=[pltpu.VMEM((n_lat + 2 * LRU_PAD, w), F32),
                        big(), big(), big(), big(), small(), small(), small(), small()],
        compiler_params=_params(1),
        name="rglru",
    )(x_lat, x_ctx, *consts)


def _merge_kernel(x_ref, mod_ref, ya_ref, yb_ref, yc_ref, yd_ref, wz_ref, wm_ref, wb_ref, wo_ref,
                  lng_ref, lnb_ref, o_ref):
    x = x_ref[0]
    shift = mod_ref[0, 0:1, :]
    scale = mod_ref[0, 1:2, :]
    gate = mod_ref[0, 2:3, :]
    xm = (_layer_norm(x) * (1.0 + scale) + shift).astype(BF16)
    z = _dot(xm, wz_ref[...])
    acc = None
    for i, y_ref in enumerate((ya_ref, yb_ref, yc_ref, yd_ref)):
        zi = z[:, BRANCH_W * i:BRANCH_W * (i + 1)]
        g = (y_ref[0].astype(F32) * (zi * _sigmoid(zi))).astype(BF16)
        t = _dot(g, wb_ref[i])
        mi = _dot(xm, wm_ref[:, D_MODEL * i:D_MODEL * (i + 1)])
        term = _sigmoid(mi) * t
        acc = term if acc is None else acc + term
    out = _dot(acc.astype(BF16), wo_ref[...])
    o_ref[0] = _layer_norm(DEEPNORM_ALPHA * x + gate * out) * lng_ref[...] + lnb_ref[...]


def _merge(x, mod, ys, wz, wm, wb, wo, ln_g, ln_b, tile):
    b, n, _ = x.shape
    consts = (wz, wm, wb, wo, ln_g, ln_b)
    tok = lambda w: pl.BlockSpec((1, tile, w), lambda bb, t: (bb, t, 0))
    return pl.pallas_call(
        _merge_kernel,
        grid=(b, n // tile),
        in_specs=[tok(D_MODEL), pl.BlockSpec((1, 3, D_MODEL), lambda bb, t: (bb, 0, 0))]
        + [tok(BRANCH_W) for _ in ys] + [_const_spec(c.shape) for c in consts],
        out_specs=tok(D_MODEL),
        out_shape=jax.ShapeDtypeStruct((b, n, D_MODEL), F32),
        compiler_params=_params(2),
        name="merge",
    )(x, mod, *ys, *consts)


def _rope_tables(n):
    t = jnp.arange(n, dtype=jnp.int32)
    rows = (t // GRID_W).astype(F32)[:, None]
    cols = (t % GRID_W).astype(F32)[:, None]

    def table(half, offset):
        inv = ROPE_THETA ** (-jnp.arange(0, 2 * half, 2, dtype=F32) / (2 * half))
        cos_parts = [jnp.ones((n, offset), F32)]
        sin_parts = [jnp.zeros((n, offset), F32)]
        for pos in (rows, cols):
            ang = pos * inv[None, :]
            c, sn = jnp.cos(ang), jnp.sin(ang)
            cos_parts += [c, c]
            sin_parts += [-sn, sn]
        rest = LANES - offset - 4 * half
        cos_parts.append(jnp.ones((n, rest), F32))
        sin_parts.append(jnp.zeros((n, rest), F32))
        return jnp.concatenate(cos_parts, -1), jnp.concatenate(sin_parts, -1)

    cd, sd = table(16, 0)
    cosd = jnp.concatenate([cd[:, :64], cd[:, :64]], -1)
    sind = jnp.concatenate([sd[:, :64], sd[:, :64]], -1)
    cosb, sinb = table(8, MLA_NOPE)
    cosk, sink = table(8, 0)
    return cosd, sind, cosb, sinb, cosk, sink


def _identity_tables(n):
    one, zero = jnp.ones((n, LANES), F32), jnp.zeros((n, LANES), F32)
    return one, zero, one, zero, one, zero


def _layer_weights(w_in, mla_q_norm, mla_w_uq, mla_kv_norm, mla_w_ukv, gqa_q_norm, gqa_k_norm, w_branch, w_out):
    wi = w_in[:, :MIX_COLS]
    qa, ka, va = wi[:, 0:256], wi[:, 256:512], wi[:, 512:768]
    cq, ckv, kr = wi[:, 768:1024], wi[:, 1024:1152], wi[:, 1152:1184]
    lru, qd, kd, vd = wi[:, 1184:1440], wi[:, 1440:1696], wi[:, 1696:1824], wi[:, 1824:1952]
    head_order = jnp.array([0, 2, 1, 3])
    qd = qd.reshape(D_MODEL, 4, HEAD_DIM)[:, head_order].reshape(D_MODEL, 256)
    wmix = jnp.concatenate([qa, ka, va, cq, ckv, lru, qd, kd, vd, kr,
                            jnp.zeros((D_MODEL, MIX_PAD - MIX_COLS), F32)], -1).astype(BF16)
    hq = MLA_NOPE + MLA_ROPE
    wuq = jnp.pad(mla_w_uq.reshape(256, MLA_HEADS, hq), ((0, 0), (0, 0), (0, LANES - hq)))
    wuq = wuq.reshape(256, MLA_HEADS * LANES).astype(BF16)
    ukv = mla_w_ukv.reshape(128, MLA_HEADS, 128)
    wuk = jnp.pad(ukv[:, :, :MLA_NOPE], ((0, 0), (0, 0), (0, LANES - MLA_NOPE)))
    wuk = wuk.reshape(128, MLA_HEADS * LANES).astype(BF16)
    wuv = ukv[:, :, MLA_NOPE:].reshape(128, 256).astype(BF16)
    pk = np.zeros((LANES, MLA_HEADS * LANES), np.float32)
    for h in range(MLA_HEADS):
        pk[np.arange(MLA_ROPE), LANES * h + MLA_NOPE + np.arange(MLA_ROPE)] = 1.0
    grp = np.kron(np.eye(4, dtype=np.float32), np.ones((HEAD_DIM, HEAD_DIM), np.float32))
    inproj_w = (wmix, wuq, wuk, wuv, jnp.asarray(pk, BF16), jnp.asarray(grp, BF16),
                mla_q_norm.reshape(1, 256), mla_kv_norm.reshape(1, 128),
                jnp.tile(gqa_q_norm, 4).reshape(1, 256), jnp.tile(gqa_k_norm, 2).reshape(1, 128))
    wz = w_in[:, MIX_COLS:MIX_COLS + SILU_COLS]
    wz_d = wz[:, 3 * BRANCH_W:].reshape(D_MODEL, 4, HEAD_DIM)[:, head_order].reshape(D_MODEL, BRANCH_W)
    wz = jnp.concatenate([wz[:, :3 * BRANCH_W], wz_d], -1).astype(BF16)
    wm = w_in[:, MIX_COLS + SILU_COLS:].astype(BF16)
    wb_d = w_branch[3].reshape(4, HEAD_DIM, D_MODEL)[head_order].reshape(BRANCH_W, D_MODEL)
    wb = jnp.concatenate([w_branch[:3], wb_d[None]], 0).astype(BF16)
    return inproj_w, (wz, wm, wb, w_out.astype(BF16))


def _block_diag(w):
    eye = jnp.eye(4, dtype=w.dtype)
    return jnp.einsum("dkce,kj->dkcje", w, eye).reshape(2, 256, 256)


def kernel(x, c, ctx, c_ctx, w_mod, b_mod, w_in, na_rel_bias, mla_q_norm, mla_w_uq, mla_kv_norm, mla_w_ukv,
           lru_conv_w, lru_conv_b, lru_w_a, lru_b_a, lru_w_x, lru_b_x, lru_lambda, gqa_q_norm, gqa_k_norm,
           w_branch, w_out, ln_g, ln_b):
    b, n, _ = x.shape
    n_ctx = ctx.shape[1]
    rows = 8 * ((b + 1 + 7) // 8)
    c_all = jnp.zeros((rows, D_MODEL), F32).at[:b].set(c).at[b].set(c_ctx)
    mod_all = _modulation(c_all, w_mod, b_mod)
    lat_tables = _rope_tables(n)
    ctx_tables = _identity_tables(n_ctx)

    for l in range(DEPTH):
        need_ctx = l < DEPTH - 1
        mod_lat = mod_all[l, :b].reshape(b, 3, D_MODEL)
        mod_ctx = jnp.broadcast_to(mod_all[l, b].reshape(1, 3, D_MODEL), (b, 3, D_MODEL))
        inproj_w, merge_w = _layer_weights(w_in[l], mla_q_norm[l], mla_w_uq[l], mla_kv_norm[l], mla_w_ukv[l],
                                           gqa_q_norm[l], gqa_k_norm[l], w_branch[l], w_out[l])
        qa, ka, va, qb, kb, vb, xr, qd, kd, vd, vat, vbt, vdt = _inproj(x, mod_lat, inproj_w, lat_tables, 512)
        qa_c, ka_c, va_c, qb_c, kb_c, vb_c, xr_c, qd_c, kd_c, vd_c, vat_c, vbt_c, vdt_c = _inproj(
            ctx, mod_ctx, inproj_w, ctx_tables, n_ctx)

        ya = _na_attention(qa, ka, vat, ka_c, vat_c, _na_bias(na_rel_bias[l]))
        yb = _attention_t(qb, kb, vbt, kb_c, vbt_c, mode="B", tq=1024)
        yd = _attention_t(qd, kd, vdt, kd_c, vdt_c, mode="D", tq=1024)
        yc, yc_c = _lru(xr, xr_c, lru_conv_w[l], lru_conv_b[l].reshape(1, -1),
                        _block_diag(lru_w_a[l]).astype(BF16), lru_b_a[l],
                        _block_diag(lru_w_x[l]).astype(BF16), lru_b_x[l], lru_lambda[l])
        tail = (*merge_w, ln_g[l].reshape(1, -1), ln_b[l].reshape(1, -1))
        x_new = _merge(x, mod_lat, (ya, yb, yc, yd), *tail, 512)
        if need_ctx:
            ya_c = _attention(qa_c, None, None, ka_c, va_c, mode="A", tq=n_ctx)
            yb_c = _attention(qb_c, None, None, kb_c, vb_c, mode="B", tq=n_ctx)
            yd_c = _attention(qd_c, None, None, kd_c, vd_c, mode="D", tq=n_ctx)
            ctx = _merge(ctx, mod_ctx, (ya_c, yb_c, yc_c, yd_c), *tail, n_ctx)
        x = x_new
    return x
```

```python
import functools
import math

import numpy as np
import jax
import jax.numpy as jnp
from jax import lax
from jax.experimental import pallas as pl
from jax.experimental.pallas import tpu as pltpu

F32 = jnp.float32
BF16 = jnp.bfloat16

D_MODEL = 1024
DEPTH = 2
GRID_W = 64
BRANCH_W = 256
HEAD_DIM = 64
WIN_H = 8
WIN_W = 16
MLA_HEADS = 4
MLA_NOPE = 64
MLA_ROPE = 32
LRU_C = 8.0
CONV_W = 4
ROPE_THETA = 10000.0
EPS = 1e-6
LOG2E = 1.4426950408889634
NA_SCALE = HEAD_DIM ** -0.5
MLA_SCALE = (MLA_NOPE + MLA_ROPE) ** -0.5
GQA_SCALE = HEAD_DIM ** -0.5
DEEPNORM_ALPHA = (2 * DEPTH) ** 0.25

MIX_COLS = 1952
SILU_COLS = 1024
LANES = 128
NEG_BIG = -1e30
VMEM_LIMIT = 56 * 1024 * 1024

OFF_QA, OFF_KA, OFF_VA, OFF_CQ, OFF_CKV, OFF_LRU, OFF_QD, OFF_KD, OFF_VD, OFF_KR = (
    0, 256, 512, 768, 1024, 1152, 1408, 1664, 1792, 1920)
MIX_PAD = 2048


def _dot(a, b):
    return jnp.dot(a, b, preferred_element_type=F32)


def _dot_t(a, b):
    return lax.dot_general(a, b, (((1,), (1,)), ((), ())), preferred_element_type=F32)


def _layer_norm(x):
    mu = jnp.mean(x, axis=-1, keepdims=True)
    xc = x - mu
    var = jnp.mean(xc * xc, axis=-1, keepdims=True)
    return xc * lax.rsqrt(var + EPS)


def _sigmoid(x):
    return 0.5 * jnp.tanh(0.5 * x) + 0.5


def _rms(x, g):
    return x * lax.rsqrt(jnp.mean(x * x, axis=-1, keepdims=True) + EPS) * g


def _params(n_grid):
    return pltpu.CompilerParams(dimension_semantics=("arbitrary",) * n_grid,
                                vmem_limit_bytes=VMEM_LIMIT)


def _const_spec(shape, single_buffer=False):
    zeros = (0,) * len(shape)
    mode = pl.Buffered(1) if single_buffer else None
    return pl.BlockSpec(shape, lambda *_: zeros, pipeline_mode=mode)


def _mod_kernel(c_ref, w_ref, b_ref, o_ref):
    c = c_ref[...]
    s = (c * jax.nn.sigmoid(c)).astype(BF16)
    o_ref[0] = _dot(s, w_ref[0].astype(BF16)) + b_ref[0]


def _modulation(c_all, w_mod, b_mod):
    n_l, _, n_out = w_mod.shape
    rows = c_all.shape[0]
    bn = 512
    return pl.pallas_call(
        _mod_kernel,
        grid=(n_l, n_out // bn),
        in_specs=[pl.BlockSpec((rows, D_MODEL), lambda l, n: (0, 0)),
                  pl.BlockSpec((1, D_MODEL, bn), lambda l, n: (l, 0, n)),
                  pl.BlockSpec((1, 1, bn), lambda l, n: (l, 0, n))],
        out_specs=pl.BlockSpec((1, rows, bn), lambda l, n: (l, 0, n)),
        out_shape=jax.ShapeDtypeStruct((n_l, rows, n_out), F32),
        compiler_params=_params(2),
        name="modulation",
    )(c_all, w_mod, b_mod.reshape(n_l, 1, n_out))


def _pair_select(lane, shift):
    lane_up = pltpu.roll(lane, LANES - shift, 1)
    want = jnp.where((lane & (2 * shift - 1)) < shift, lane + shift, lane - shift)
    return lane_up == want


def _rope(x, cos, sin, shift, sel):
    partner = jnp.where(sel, pltpu.roll(x, LANES - shift, 1), pltpu.roll(x, shift, 1))
    return x * cos + partner * sin


def _group_sumsq(x, g):
    sq = x * x
    hi = sq.astype(BF16)
    lo = (sq - hi.astype(F32)).astype(BF16)
    return _dot(hi, g) + _dot(lo, g)


def _inproj_kernel(x_ref, mod_ref, wmix_ref, wuq_ref, wuk_ref, wuv_ref, pk_ref, g_ref,
                   gq_ref, gkv_ref, gdq_ref, gdk_ref,
                   cosd_ref, sind_ref, cosb_ref, sinb_ref, cosk_ref, sink_ref,
                   qa_o, ka_o, va_o, qb_o, kb_o, vb_o, xc_o, qd_o, kd_o, vd_o, vat_o, vbt_o, vdt_o):
    x = x_ref[0]
    shift = mod_ref[0, 0:1, :]
    scale = mod_ref[0, 1:2, :]
    xm = (_layer_norm(x) * (1.0 + scale) + shift).astype(BF16)
    p = _dot(xm, wmix_ref[...])
    t = x.shape[0]
    lane = lax.broadcasted_iota(jnp.int32, (t, LANES), 1)
    sel16 = _pair_select(lane, 16)
    sel8 = _pair_select(lane, 8)

    qa_o[0] = (p[:, OFF_QA:OFF_QA + 256] * (NA_SCALE * LOG2E)).astype(BF16)
    ka_o[0] = p[:, OFF_KA:OFF_KA + 256].astype(BF16)
    va = p[:, OFF_VA:OFF_VA + 256]
    va_o[0] = va.astype(BF16)
    vat_o[0] = va.T.astype(BF16)

    cqn = _rms(p[:, OFF_CQ:OFF_CQ + 256], gq_ref[...]).astype(BF16)
    qb = _dot(cqn, wuq_ref[...])
    cosb, sinb = cosb_ref[...], sinb_ref[...]
    for h in range(MLA_HEADS):
        blk = _rope(qb[:, LANES * h:LANES * (h + 1)], cosb, sinb, 8, sel8)
        qb_o[0, :, LANES * h:LANES * (h + 1)] = (blk * (MLA_SCALE * LOG2E)).astype(BF16)
    ckvn = _rms(p[:, OFF_CKV:OFF_CKV + 128], gkv_ref[...]).astype(BF16)
    kr = _rope(p[:, OFF_KR:OFF_KR + 128], cosk_ref[...], sink_ref[...], 8, sel8).astype(BF16)
    kb_o[0] = (_dot(ckvn, wuk_ref[...]) + _dot(kr, pk_ref[...])).astype(BF16)
    vb = _dot(ckvn, wuv_ref[...])
    vb_o[0] = vb.astype(BF16)
    vbt_o[0] = vb.T.astype(BF16)

    xc_o[0] = p[:, OFF_LRU:OFF_LRU + 256]

    cosd, sind = cosd_ref[...], sind_ref[...]
    qd = p[:, OFF_QD:OFF_QD + 256]
    qd = qd * lax.rsqrt(_group_sumsq(qd, g_ref[...]) * (1.0 / HEAD_DIM) + EPS) * gdq_ref[...]
    for blk_i in range(2):
        blk = _rope(qd[:, LANES * blk_i:LANES * (blk_i + 1)], cosd, sind, 16, sel16)
        qd_o[0, :, LANES * blk_i:LANES * (blk_i + 1)] = (blk * (GQA_SCALE * LOG2E)).astype(BF16)
    kd = p[:, OFF_KD:OFF_KD + 128]
    kd = kd * lax.rsqrt(_group_sumsq(kd, g_ref[0:128, 0:128]) * (1.0 / HEAD_DIM) + EPS) * gdk_ref[...]
    kd_o[0] = _rope(kd, cosd, sind, 16, sel16).astype(BF16)
    vd = p[:, OFF_VD:OFF_VD + 128]
    vd_o[0] = vd.astype(BF16)
    vdt_o[0] = vd.T.astype(BF16)


def _inproj(x, mod, weights, tables, tile):
    b, n, _ = x.shape
    nt = n // tile
    wspecs = [_const_spec(w.shape) for w in weights]
    tspecs = [pl.BlockSpec((tile, LANES), lambda t, bb: (t, 0)) for _ in tables]
    widths = (256, 256, 256, 512, 512, 256, 256, 256, 128, 128)
    dtypes = (BF16, BF16, BF16, BF16, BF16, BF16, F32, BF16, BF16, BF16)
    t_rows = (256, 256, 128)
    return pl.pallas_call(
        _inproj_kernel,
        grid=(nt, b),
        in_specs=[pl.BlockSpec((1, tile, D_MODEL), lambda t, bb: (bb, t, 0)),
                  pl.BlockSpec((1, 3, D_MODEL), lambda t, bb: (bb, 0, 0))] + wspecs + tspecs,
        out_specs=[pl.BlockSpec((1, tile, w), lambda t, bb: (bb, t, 0)) for w in widths]
        + [pl.BlockSpec((1, r, tile), lambda t, bb: (bb, 0, t)) for r in t_rows],
        out_shape=[jax.ShapeDtypeStruct((b, n, w), dt) for w, dt in zip(widths, dtypes)]
        + [jax.ShapeDtypeStruct((b, r, n), BF16) for r in t_rows],
        compiler_params=_params(2),
        name="inproj",
    )(x, mod, *weights, *tables)


def _half_mask(lane, hh):
    return lane >= HEAD_DIM if hh else lane < HEAD_DIM


def _chunk_update(qh, kc, vc, m, l, acc):
    s = _dot_t(qh, kc)
    m_new = jnp.maximum(m, jnp.max(s, axis=-1, keepdims=True))
    alpha = jnp.exp2(m - m_new)
    pr = jnp.exp2(s - m_new)
    l = alpha * l + jnp.sum(pr, axis=-1, keepdims=True)
    acc = alpha * acc + _dot(pr.astype(BF16), vc)
    return m_new, l, acc


def _attn_kernel(*refs, masked, n_lat, chunk):
    if n_lat:
        q_ref, kl_ref, vl_ref, kc_ref, vc_ref, o_ref = refs
    else:
        q_ref, kc_ref, vc_ref, o_ref = refs
    tq = q_ref.shape[1]
    lane = lax.broadcasted_iota(jnp.int32, (tq, LANES), 1)
    qs, ksls = [], []
    for hh in range(2):
        if masked:
            qs.append(jnp.where(_half_mask(lane, hh), q_ref[0], jnp.zeros((), BF16)))
            ksls.append(slice(None))
        else:
            qs.append(q_ref[0, :, LANES * hh:LANES * (hh + 1)])
            ksls.append(slice(LANES * hh, LANES * (hh + 1)))
    carries = [(jnp.full((tq, 1), NEG_BIG, F32), jnp.zeros((tq, 1), F32), jnp.zeros((tq, LANES), F32))
               for _ in range(2)]
    for c0 in range(0, n_lat, chunk):
        rows = pl.ds(c0, chunk)
        for hh in range(2):
            carries[hh] = _chunk_update(qs[hh], kl_ref[0, rows, ksls[hh]], vl_ref[0, rows, :], *carries[hh])
    outs = []
    for hh in range(2):
        m, l, acc = _chunk_update(qs[hh], kc_ref[0, :, ksls[hh]], vc_ref[0], *carries[hh])
        outs.append(acc * (1.0 / l))
    o_ref[0] = jnp.where(lane < HEAD_DIM, outs[0], outs[1]).astype(BF16)


def _attention(q, k_lat, v_lat, k_ctx, v_ctx, *, mode, tq, chunk=1024):
    b, nq, _ = q.shape
    masked = mode != "B"
    qw = 256 if mode == "B" else LANES
    kw = 256 if mode == "B" else LANES
    kidx = (lambda bb, p, t: (bb, 0, 0)) if mode == "D" else (lambda bb, p, t: (bb, 0, p))
    n_ctx = k_ctx.shape[1]
    n_lat = 0 if k_lat is None else k_lat.shape[1]
    in_specs = [pl.BlockSpec((1, tq, qw), lambda bb, p, t: (bb, t, p))]
    args = [q]
    if n_lat:
        in_specs += [pl.BlockSpec((1, n_lat, kw), kidx), pl.BlockSpec((1, n_lat, LANES), kidx)]
        args += [k_lat, v_lat]
    in_specs += [pl.BlockSpec((1, n_ctx, kw), kidx), pl.BlockSpec((1, n_ctx, LANES), kidx)]
    args += [k_ctx, v_ctx]
    return pl.pallas_call(
        functools.partial(_attn_kernel, masked=masked, n_lat=n_lat, chunk=chunk),
        grid=(b, 2, nq // tq),
        in_specs=in_specs,
        out_specs=pl.BlockSpec((1, tq, LANES), lambda bb, p, t: (bb, t, p)),
        out_shape=jax.ShapeDtypeStruct((b, nq, 256), BF16),
        compiler_params=_params(3),
        name="attn_" + mode + ("_lat" if n_lat else "_ctx"),
    )(*args)


def _with_ones_row(vt, hh):
    row = lax.broadcasted_iota(jnp.int32, vt.shape, 0)
    return jnp.where(row == _ones_row(hh), jnp.ones((), BF16), vt)


def _ones_row(hh):
    return 0 if hh else HEAD_DIM


def _softmax_update_t(s, vtc, m, acc):
    m_new = jnp.maximum(m, jnp.max(s, axis=0, keepdims=True))
    alpha = jnp.exp2(m - m_new)
    pr = jnp.exp2(s - m_new)
    acc = alpha * acc + _dot(vtc, pr.astype(BF16))
    return m_new, acc


def _finish_t(accs):
    outs = [acc * (1.0 / acc[_ones_row(hh):_ones_row(hh) + 1, :]) for hh, acc in enumerate(accs)]
    row = lax.broadcasted_iota(jnp.int32, outs[0].shape, 0)
    return jnp.where(row < HEAD_DIM, outs[0], outs[1]).T.astype(BF16)


def _attn_t_kernel(q_ref, kl_ref, vtl_ref, kc_ref, vtc_ref, o_ref, *, masked, chunk):
    tq = q_ref.shape[1]
    n_lat = kl_ref.shape[1]
    lane = lax.broadcasted_iota(jnp.int32, (tq, LANES), 1)
    qs, ksls = [], []
    for hh in range(2):
        if masked:
            qs.append(jnp.where(_half_mask(lane, hh), q_ref[0], jnp.zeros((), BF16)))
            ksls.append(slice(None))
        else:
            qs.append(q_ref[0, :, LANES * hh:LANES * (hh + 1)])
            ksls.append(slice(LANES * hh, LANES * (hh + 1)))
    carries = [(jnp.full((1, tq), NEG_BIG, F32), jnp.zeros((LANES, tq), F32)) for _ in range(2)]
    tasks = [(hh, (kl_ref, pl.ds(c0, chunk)), (vtl_ref, pl.ds(c0, chunk)))
             for c0 in range(0, n_lat, chunk) for hh in range(2)]
    tasks += [(hh, (kc_ref, slice(None)), (vtc_ref, slice(None))) for hh in range(2)]

    def scores(task):
        hh, (k_ref, rows), _ = task
        return _dot_t(k_ref[0, rows, ksls[hh]], qs[hh])

    s_next = scores(tasks[0])
    for i, task in enumerate(tasks):
        s_cur = s_next
        if i + 1 < len(tasks):
            s_next = scores(tasks[i + 1])
        hh, _, (vt_ref, cols) = task
        carries[hh] = _softmax_update_t(s_cur, _with_ones_row(vt_ref[0, :, cols], hh), *carries[hh])
    o_ref[0] = _finish_t([acc for _, acc in carries])


def _attention_t(q, k_lat, vt_lat, k_ctx, vt_ctx, *, mode, tq, chunk=1024):
    b, nq, _ = q.shape
    masked = mode != "B"
    qw = 256 if mode == "B" else LANES
    kidx = (lambda bb, p, t: (bb, 0, 0)) if mode == "D" else (lambda bb, p, t: (bb, 0, p))
    vidx = (lambda bb, p, t: (bb, 0, 0)) if mode == "D" else (lambda bb, p, t: (bb, p, 0))
    n_lat, n_ctx = k_lat.shape[1], k_ctx.shape[1]
    return pl.pallas_call(
        functools.partial(_attn_t_kernel, masked=masked, chunk=chunk),
        grid=(b, 2, nq // tq),
        in_specs=[pl.BlockSpec((1, tq, qw), lambda bb, p, t: (bb, t, p)),
                  pl.BlockSpec((1, n_lat, qw), kidx), pl.BlockSpec((1, LANES, n_lat), vidx),
                  pl.BlockSpec((1, n_ctx, qw), kidx), pl.BlockSpec((1, LANES, n_ctx), vidx)],
        out_specs=pl.BlockSpec((1, tq, LANES), lambda bb, p, t: (bb, t, p)),
        out_shape=jax.ShapeDtypeStruct((b, nq, 256), BF16),
        compiler_params=_params(3),
        name="attn_" + mode + "_lat",
    )(q, k_lat, vt_lat, k_ctx, vt_ctx)


NA_QROWS = 8
NA_KROWS = 16
NA_TQ = NA_QROWS * GRID_W
NA_KBLK = 256


def _na_kernel(q_ref, k0, k1, k2, k3, vt0, vt1, vt2, vt3, kc_ref, vtc_ref, bias_ref, o_ref):
    q = q_ref[0]
    lane = lax.broadcasted_iota(jnp.int32, (NA_TQ, LANES), 1)
    k_win = jnp.concatenate([k0[0], k1[0], k2[0], k3[0]], axis=0)
    vt_win = jnp.concatenate([vt0[0], vt1[0], vt2[0], vt3[0]], axis=1)
    qs = [jnp.where(_half_mask(lane, hh), q, jnp.zeros((), BF16)) for hh in range(2)]
    scores = [(_dot_t(k_win, qh) + bias_ref[hh, 0], _dot_t(kc_ref[0], qh)) for hh, qh in enumerate(qs)]
    accs = []
    for hh, (s_win, s_ctx) in enumerate(scores):
        m = jnp.maximum(jnp.max(s_win, axis=0, keepdims=True), jnp.max(s_ctx, axis=0, keepdims=True))
        p_win = jnp.exp2(s_win - m).astype(BF16)
        p_ctx = jnp.exp2(s_ctx - m).astype(BF16)
        accs.append(_dot(_with_ones_row(vt_win, hh), p_win) + _dot(_with_ones_row(vtc_ref[0], hh), p_ctx))
    o_ref[0] = _finish_t(accs)


def _na_bias(rel_bias):
    h, n_dr, n_dc = rel_bias.shape
    n_rows = 64
    w2 = 2 * GRID_W
    pad_lo = GRID_W - WIN_W
    p = jnp.pad(rel_bias * LOG2E, ((0, 0), (0, 0), (pad_lo, w2 - pad_lo - n_dc)))
    skew = jnp.broadcast_to(p[:, :, None, :], (h, n_dr, GRID_W, w2)).reshape(h, n_dr, GRID_W * w2)
    skew = skew[:, :, :GRID_W * (w2 - 1)].reshape(h, n_dr, GRID_W, w2 - 1)
    tiles = skew[..., GRID_W - 1:]
    qc = np.arange(GRID_W)
    cs = np.clip(qc - WIN_W // 2, 0, GRID_W - WIN_W)
    in_col = (qc[None, :] >= cs[:, None]) & (qc[None, :] < cs[:, None] + WIN_W)
    tiles = jnp.where(in_col, tiles, NEG_BIG)
    seq = tiles.transpose(0, 2, 1, 3).reshape(h, GRID_W, n_dr * GRID_W)
    variants = []
    for j in (0, 1, n_rows // NA_QROWS - 1):
        kb = int(np.clip(NA_QROWS * j - WIN_H // 2, 0, n_rows - NA_KROWS))
        strips = []
        for i in range(NA_QROWS):
            qr = NA_QROWS * j + i
            rs = int(np.clip(qr - WIN_H // 2, 0, n_rows - WIN_H))
            lead = rs - kb
            a0 = rs - qr + WIN_H - 1
            tail = NA_KROWS - WIN_H - lead
            strips.append(jnp.concatenate(
                [jnp.full((h, GRID_W, lead * GRID_W), NEG_BIG, F32),
                 seq[:, :, a0 * GRID_W:(a0 + WIN_H) * GRID_W],
                 jnp.full((h, GRID_W, tail * GRID_W), NEG_BIG, F32)], axis=-1))
        variants.append(jnp.concatenate(strips, axis=1))
    return jnp.swapaxes(jnp.stack(variants, axis=1), -1, -2)


def _na_attention(q, k, vt, k_ctx, vt_ctx, bias):
    b, n, _ = q.shape
    nj = n // NA_TQ
    n_kblk = n // NA_KBLK
    n_ctx = k_ctx.shape[1]

    def kstart(j):
        return jnp.clip(2 * j - 1, 0, n_kblk - 4)

    def kspec(i):
        return pl.BlockSpec((1, NA_KBLK, LANES), lambda p, j, bb: (bb, kstart(j) + i, p))

    def vspec(i):
        return pl.BlockSpec((1, LANES, NA_KBLK), lambda p, j, bb: (bb, p, kstart(j) + i))

    def variant(j):
        return jnp.where(j == 0, 0, jnp.where(j == nj - 1, 2, 1))

    return pl.pallas_call(
        _na_kernel,
        grid=(2, nj, b),
        in_specs=[pl.BlockSpec((1, NA_TQ, LANES), lambda p, j, bb: (bb, j, p))]
        + [kspec(i) for i in range(4)] + [vspec(i) for i in range(4)]
        + [pl.BlockSpec((1, n_ctx, LANES), lambda p, j, bb: (bb, 0, p)),
           pl.BlockSpec((1, LANES, n_ctx), lambda p, j, bb: (bb, p, 0)),
           pl.BlockSpec((2, 1, NA_KROWS * GRID_W, NA_TQ), lambda p, j, bb: (p, variant(j), 0, 0))],
        out_specs=pl.BlockSpec((1, NA_TQ, LANES), lambda p, j, bb: (bb, j, p)),
        out_shape=jax.ShapeDtypeStruct((b, n, 256), BF16),
        compiler_params=_params(3),
        name="na_lat",
    )(q, k, k, k, k, vt, vt, vt, vt, k_ctx, vt_ctx, bias)


LRU_SEG = 8
LRU_PAD = 8
LRU_ROWS = 512
LRU_HALVES = 2


def _pitch(n):
    seg = n // LRU_SEG
    return seg + 8 if (seg // 8) % 2 == 0 else seg


def _pieces(n, t0, r):
    seg, out, t = n // LRU_SEG, [], t0
    while t < t0 + r:
        length = min(seg - t % seg, t0 + r - t)
        out.append((t - t0, (t // seg) * _pitch(n) + t % seg, length))
        t += length
    return out


def _put(ref, n, t0, val):
    for off, row, length in _pieces(n, t0, val.shape[0]):
        for hv in range(LRU_HALVES):
            ref[hv, pl.ds(row, length), :] = val[off:off + length, LANES * hv:LANES * (hv + 1)]


def _get(ref, n, t0, r):
    return jnp.concatenate(
        [jnp.concatenate([ref[hv, pl.ds(row, length), :] for hv in range(LRU_HALVES)], axis=-1)
         for _, row, length in _pieces(n, t0, r)], axis=0)


def _lru_gates(xp, n, cw, cb, wa_ref, ba_ref, wx_ref, bx_ref, sp, a_refs, u_refs):
    r = min(LRU_ROWS, n)
    for c0 in range(0, n, r):
        conv = cb
        for tap in range(CONV_W):
            conv = conv + cw[tap:tap + 1, :] * xp[pl.ds(LRU_PAD + c0 + tap - CONV_W // 2, r), :]
        xb = conv.astype(BF16)
        for d in range(2):
            rg = _sigmoid(_dot(xb, wa_ref[d]) + ba_ref[d:d + 1, :])
            ig = _sigmoid(_dot(xb, wx_ref[d]) + bx_ref[d:d + 1, :])
            a = jnp.exp((-LRU_C) * rg * sp[d:d + 1, :])
            _put(a_refs[d], n, c0, a)
            _put(u_refs[d], n, c0, jnp.sqrt(1.0 - a * a) * (ig * conv))


def _lru_scan(n, af, uf, ab, ub, hf0, hb0):
    seg, pitch = n // LRU_SEG, _pitch(n)

    def body(i, carry):
        rf = pl.ds(i, LRU_SEG, stride=pitch)
        rb = pl.ds(seg - 1 - i, LRU_SEG, stride=pitch)
        out = []
        for hv in range(LRU_HALVES):
            hlf, cmf, hlb, cmb = carry[4 * hv:4 * hv + 4]
            a1, u1 = af[hv, rf, :], uf[hv, rf, :]
            a2, u2 = ab[hv, rb, :], ub[hv, rb, :]
            hlf = a1 * hlf + u1
            cmf = a1 * cmf
            hlb = a2 * hlb + u2
            cmb = a2 * cmb
            uf[hv, rf, :] = hlf
            af[hv, rf, :] = cmf
            ub[hv, rb, :] = hlb
            ab[hv, rb, :] = cmb
            out += [hlf, cmf, hlb, cmb]
        return tuple(out)

    zero = jnp.zeros((LRU_SEG, LANES), F32)
    one = jnp.ones((LRU_SEG, LANES), F32)
    lax.fori_loop(0, seg, body, (zero, one, zero, one) * LRU_HALVES)
    hf, hb = list(hf0), list(hb0)
    for s in range(LRU_SEG):
        rf = pl.ds(s * pitch, seg)
        rb = pl.ds((LRU_SEG - 1 - s) * pitch, seg)
        for hv in range(LRU_HALVES):
            h = uf[hv, rf, :] + af[hv, rf, :] * hf[hv]
            uf[hv, rf, :] = h
            hf[hv] = h[seg - 1:seg, :]
            h = ub[hv, rb, :] + ab[hv, rb, :] * hb[hv]
            ub[hv, rb, :] = h
            hb[hv] = h[0:1, :]
    return hf, hb


def _lru_kernel(xl_ref, xc_ref, cw_ref, cb_ref, wa_ref, ba_ref, wx_ref, bx_ref, lam_ref,
                yl_ref, yc_ref, xp, afl, ufl, abl, ubl, afc, ufc, abc, ubc):
    n_lat, w = xl_ref.shape[1], xl_ref.shape[2]
    n_ctx = xc_ref.shape[1]
    cw, cb = cw_ref[...], cb_ref[...]
    neg_lam = -lam_ref[...]
    sp = jnp.maximum(neg_lam, 0.0) + jnp.log1p(jnp.exp(-jnp.abs(neg_lam)))
    zpad = jnp.zeros((LRU_PAD, w), F32)

    xp[pl.ds(0, LRU_PAD), :] = zpad
    xp[pl.ds(LRU_PAD, n_ctx), :] = xc_ref[0]
    xp[pl.ds(LRU_PAD + n_ctx, LRU_PAD), :] = zpad
    _lru_gates(xp, n_ctx, cw, cb, wa_ref, ba_ref, wx_ref, bx_ref, sp, (afc, abc), (ufc, ubc))
    xp[pl.ds(LRU_PAD, n_lat), :] = xl_ref[0]
    xp[pl.ds(LRU_PAD + n_lat, LRU_PAD), :] = zpad
    _lru_gates(xp, n_lat, cw, cb, wa_ref, ba_ref, wx_ref, bx_ref, sp, (afl, abl), (ufl, ubl))

    h0 = [jnp.zeros((1, LANES), F32)] * LRU_HALVES
    hf, hb = _lru_scan(n_ctx, afc, ufc, abc, ubc, h0, h0)
    _lru_scan(n_lat, afl, ufl, abl, ubl, hf, hb)
    yc_ref[0] = (_get(ufc, n_ctx, 0, n_ctx) + _get(ubc, n_ctx, 0, n_ctx)).astype(BF16)
    for c0 in range(0, n_lat, LRU_ROWS):
        yl_ref[0, pl.ds(c0, LRU_ROWS), :] = (_get(ufl, n_lat, c0, LRU_ROWS)
                                             + _get(ubl, n_lat, c0, LRU_ROWS)).astype(BF16)


def _lru(x_lat, x_ctx, conv_w, conv_b, wa, ba, wx, bx, lam):
    b, n_lat, w = x_lat.shape
    n_ctx = x_ctx.shape[1]
    consts = (conv_w, conv_b, wa, ba, wx, bx, lam)
    big = lambda: pltpu.VMEM((LRU_HALVES, LRU_SEG * _pitch(n_lat), LANES), F32)
    small = lambda: pltpu.VMEM((LRU_HALVES, LRU_SEG * _pitch(n_ctx), LANES), F32)
    return pl.pallas_call(
        _lru_kernel,
        grid=(b,),
        in_specs=[pl.BlockSpec((1, n_lat, w), lambda bb: (bb, 0, 0)),
                  pl.BlockSpec((1, n_ctx, w), lambda bb: (bb, 0, 0))]
        + [_const_spec(c.shape) for c in consts],
        out_specs=[pl.BlockSpec((1, n_lat, w), lambda bb: (bb, 0, 0)),
                   pl.BlockSpec((1, n_ctx, w), lambda bb: (bb, 0, 0))],
        out_shape=[jax.ShapeDtypeStruct((b, n_lat, w), BF16),
                   jax.ShapeDtypeStruct((b, n_ctx, w), BF16)],
        scratch_shapes=[pltpu.VMEM((n_lat + 2 * LRU_PAD, w), F32),
                        big(), big(), big(), big(), small(), small(), small(), small()],
        compiler_params=_params(1),
        name="rglru",
    )(x_lat, x_ctx, *consts)


MERGE_SUBTILES = 2


def _merge_kernel(x_ref, mod_ref, ya_ref, yb_ref, yc_ref, yd_ref, wz_ref, wm_ref, wb_ref, wo_ref,
                  lng_ref, lnb_ref, o_ref):
    shift = mod_ref[0, 0:1, :]
    scale = mod_ref[0, 1:2, :]
    gate = mod_ref[0, 2:3, :]
    tile = x_ref.shape[1]
    sub = tile // MERGE_SUBTILES
    for r0 in range(0, tile, sub):
        rows = pl.ds(r0, sub)
        x = x_ref[0, rows, :]
        xm = (_layer_norm(x) * (1.0 + scale) + shift).astype(BF16)
        z = _dot(xm, wz_ref[...])
        acc = None
        for i, y_ref in enumerate((ya_ref, yb_ref, yc_ref, yd_ref)):
            zi = z[:, BRANCH_W * i:BRANCH_W * (i + 1)]
            g = (y_ref[0, rows, :].astype(F32) * (zi * _sigmoid(zi))).astype(BF16)
            t = _dot(g, wb_ref[i])
            mi = _dot(xm, wm_ref[:, D_MODEL * i:D_MODEL * (i + 1)])
            term = _sigmoid(mi) * t
            acc = term if acc is None else acc + term
        out = _dot(acc.astype(BF16), wo_ref[...])
        o_ref[0, rows, :] = _layer_norm(DEEPNORM_ALPHA * x + gate * out) * lng_ref[...] + lnb_ref[...]


def _merge(x, mod, ys, wz, wm, wb, wo, ln_g, ln_b, tile):
    b, n, _ = x.shape
    consts = (wz, wm, wb, wo, ln_g, ln_b)
    tok = lambda w: pl.BlockSpec((1, tile, w), lambda bb, t: (bb, t, 0))
    return pl.pallas_call(
        _merge_kernel,
        grid=(b, n // tile),
        in_specs=[tok(D_MODEL), pl.BlockSpec((1, 3, D_MODEL), lambda bb, t: (bb, 0, 0))]
        + [tok(BRANCH_W) for _ in ys] + [_const_spec(c.shape, single_buffer=True) for c in consts],
        out_specs=tok(D_MODEL),
        out_shape=jax.ShapeDtypeStruct((b, n, D_MODEL), F32),
        compiler_params=_params(2),
        name="merge",
    )(x, mod, *ys, *consts)


def _rope_tables(n):
    t = jnp.arange(n, dtype=jnp.int32)
    rows = (t // GRID_W).astype(F32)[:, None]
    cols = (t % GRID_W).astype(F32)[:, None]

    def table(half, offset):
        inv = ROPE_THETA ** (-jnp.arange(0, 2 * half, 2, dtype=F32) / (2 * half))
        cos_parts = [jnp.ones((n, offset), F32)]
        sin_parts = [jnp.zeros((n, offset), F32)]
        for pos in (rows, cols):
            ang = pos * inv[None, :]
            c, sn = jnp.cos(ang), jnp.sin(ang)
            cos_parts += [c, c]
            sin_parts += [-sn, sn]
        rest = LANES - offset - 4 * half
        cos_parts.append(jnp.ones((n, rest), F32))
        sin_parts.append(jnp.zeros((n, rest), F32))
        return jnp.concatenate(cos_parts, -1), jnp.concatenate(sin_parts, -1)

    cd, sd = table(16, 0)
    cosd = jnp.concatenate([cd[:, :64], cd[:, :64]], -1)
    sind = jnp.concatenate([sd[:, :64], sd[:, :64]], -1)
    cosb, sinb = table(8, MLA_NOPE)
    cosk, sink = table(8, 0)
    return cosd, sind, cosb, sinb, cosk, sink


def _identity_tables(n):
    one, zero = jnp.ones((n, LANES), F32), jnp.zeros((n, LANES), F32)
    return one, zero, one, zero, one, zero


def _layer_weights(w_in, mla_q_norm, mla_w_uq, mla_kv_norm, mla_w_ukv, gqa_q_norm, gqa_k_norm, w_branch, w_out):
    wi = w_in[:, :MIX_COLS]
    qa, ka, va = wi[:, 0:256], wi[:, 256:512], wi[:, 512:768]
    cq, ckv, kr = wi[:, 768:1024], wi[:, 1024:1152], wi[:, 1152:1184]
    lru, qd, kd, vd = wi[:, 1184:1440], wi[:, 1440:1696], wi[:, 1696:1824], wi[:, 1824:1952]
    head_order = jnp.array([0, 2, 1, 3])
    qd = qd.reshape(D_MODEL, 4, HEAD_DIM)[:, head_order].reshape(D_MODEL, 256)
    wmix = jnp.concatenate([qa, ka, va, cq, ckv, lru, qd, kd, vd, kr,
                            jnp.zeros((D_MODEL, MIX_PAD - MIX_COLS), F32)], -1).astype(BF16)
    hq = MLA_NOPE + MLA_ROPE
    wuq = jnp.pad(mla_w_uq.reshape(256, MLA_HEADS, hq), ((0, 0), (0, 0), (0, LANES - hq)))
    wuq = wuq.reshape(256, MLA_HEADS * LANES).astype(BF16)
    ukv = mla_w_ukv.reshape(128, MLA_HEADS, 128)
    wuk = jnp.pad(ukv[:, :, :MLA_NOPE], ((0, 0), (0, 0), (0, LANES - MLA_NOPE)))
    wuk = wuk.reshape(128, MLA_HEADS * LANES).astype(BF16)
    wuv = ukv[:, :, MLA_NOPE:].reshape(128, 256).astype(BF16)
    pk = np.zeros((LANES, MLA_HEADS * LANES), np.float32)
    for h in range(MLA_HEADS):
        pk[np.arange(MLA_ROPE), LANES * h + MLA_NOPE + np.arange(MLA_ROPE)] = 1.0
    grp = np.kron(np.eye(4, dtype=np.float32), np.ones((HEAD_DIM, HEAD_DIM), np.float32))
    inproj_w = (wmix, wuq, wuk, wuv, jnp.asarray(pk, BF16), jnp.asarray(grp, BF16),
                mla_q_norm.reshape(1, 256), mla_kv_norm.reshape(1, 128),
                jnp.tile(gqa_q_norm, 4).reshape(1, 256), jnp.tile(gqa_k_norm, 2).reshape(1, 128))
    wz = w_in[:, MIX_COLS:MIX_COLS + SILU_COLS]
    wz_d = wz[:, 3 * BRANCH_W:].reshape(D_MODEL, 4, HEAD_DIM)[:, head_order].reshape(D_MODEL, BRANCH_W)
    wz = jnp.concatenate([wz[:, :3 * BRANCH_W], wz_d], -1).astype(BF16)
    wm = w_in[:, MIX_COLS + SILU_COLS:].astype(BF16)
    wb_d = w_branch[3].reshape(4, HEAD_DIM, D_MODEL)[head_order].reshape(BRANCH_W, D_MODEL)
    wb = jnp.concatenate([w_branch[:3], wb_d[None]], 0).astype(BF16)
    return inproj_w, (wz, wm, wb, w_out.astype(BF16))


def _block_diag(w):
    eye = jnp.eye(4, dtype=w.dtype)
    return jnp.einsum("dkce,kj->dkcje", w, eye).reshape(2, 256, 256)


def kernel(x, c, ctx, c_ctx, w_mod, b_mod, w_in, na_rel_bias, mla_q_norm, mla_w_uq, mla_kv_norm, mla_w_ukv,
           lru_conv_w, lru_conv_b, lru_w_a, lru_b_a, lru_w_x, lru_b_x, lru_lambda, gqa_q_norm, gqa_k_norm,
           w_branch, w_out, ln_g, ln_b):
    b, n, _ = x.shape
    n_ctx = ctx.shape[1]
    rows = 8 * ((b + 1 + 7) // 8)
    c_all = jnp.zeros((rows, D_MODEL), F32).at[:b].set(c).at[b].set(c_ctx)
    mod_all = _modulation(c_all, w_mod, b_mod)
    lat_tables = _rope_tables(n)
    ctx_tables = _identity_tables(n_ctx)

    for l in range(DEPTH):
        need_ctx = l < DEPTH - 1
        mod_lat = mod_all[l, :b].reshape(b, 3, D_MODEL)
        mod_ctx = jnp.broadcast_to(mod_all[l, b].reshape(1, 3, D_MODEL), (b, 3, D_MODEL))
        inproj_w, merge_w = _layer_weights(w_in[l], mla_q_norm[l], mla_w_uq[l], mla_kv_norm[l], mla_w_ukv[l],
                                           gqa_q_norm[l], gqa_k_norm[l], w_branch[l], w_out[l])
        qa, ka, va, qb, kb, vb, xr, qd, kd, vd, vat, vbt, vdt = _inproj(x, mod_lat, inproj_w, lat_tables, 512)
        qa_c, ka_c, va_c, qb_c, kb_c, vb_c, xr_c, qd_c, kd_c, vd_c, vat_c, vbt_c, vdt_c = _inproj(
            ctx, mod_ctx, inproj_w, ctx_tables, n_ctx)

        ya = _na_attention(qa, ka, vat, ka_c, vat_c, _na_bias(na_rel_bias[l]))
        yb = _attention_t(qb, kb, vbt, kb_c, vbt_c, mode="B", tq=1024)
        yd = _attention_t(qd, kd, vdt, kd_c, vdt_c, mode="D", tq=1024)
        yc, yc_c = _lru(xr, xr_c, lru_conv_w[l], lru_conv_b[l].reshape(1, -1),
                        _block_diag(lru_w_a[l]).astype(BF16), lru_b_a[l],
                        _block_diag(lru_w_x[l]).astype(BF16), lru_b_x[l], lru_lambda[l])
        tail = (*merge_w, ln_g[l].reshape(1, -1), ln_b[l].reshape(1, -1))
        x_new = _merge(x, mod_lat, (ya, yb, yc, yd), *tail, 1024)
        if need_ctx:
            ya_c = _attention(qa_c, None, None, ka_c, va_c, mode="A", tq=n_ctx)
            yb_c = _attention(qb_c, None, None, kb_c, vb_c, mode="B", tq=n_ctx)
            yd_c = _attention(qd_c, None, None, kd_c, vd_c, mode="D", tq=n_ctx)
            ctx = _merge(ctx, mod_ctx, (ya_c, yb_c, yc_c, yd_c), *tail, n_ctx)
        x = x_new
    return x
```

```python
import functools
import math

import numpy as np
import jax
import jax.numpy as jnp
from jax import lax
from jax.experimental import pallas as pl
from jax.experimental.pallas import tpu as pltpu

F32 = jnp.float32
BF16 = jnp.bfloat16

D_MODEL = 1024
DEPTH = 2
GRID_W = 64
BRANCH_W = 256
HEAD_DIM = 64
WIN_H = 8
WIN_W = 16
MLA_HEADS = 4
MLA_NOPE = 64
MLA_ROPE = 32
LRU_C = 8.0
CONV_W = 4
ROPE_THETA = 10000.0
EPS = 1e-6
LOG2E = 1.4426950408889634
NA_SCALE = HEAD_DIM ** -0.5
MLA_SCALE = (MLA_NOPE + MLA_ROPE) ** -0.5
GQA_SCALE = HEAD_DIM ** -0.5
DEEPNORM_ALPHA = (2 * DEPTH) ** 0.25

MIX_COLS = 1952
SILU_COLS = 1024
LANES = 128
NEG_BIG = -1e30
VMEM_LIMIT = 56 * 1024 * 1024

OFF_QA, OFF_KA, OFF_VA, OFF_CQ, OFF_CKV, OFF_LRU, OFF_QD, OFF_KD, OFF_VD, OFF_KR = (
    0, 256, 512, 768, 1024, 1152, 1408, 1664, 1792, 1920)
MIX_PAD = 2048


def _dot(a, b):
    return jnp.dot(a, b, preferred_element_type=F32)


def _dot_t(a, b):
    return lax.dot_general(a, b, (((1,), (1,)), ((), ())), preferred_element_type=F32)


def _layer_norm(x):
    mu = jnp.mean(x, axis=-1, keepdims=True)
    xc = x - mu
    var = jnp.mean(xc * xc, axis=-1, keepdims=True)
    return xc * lax.rsqrt(var + EPS)


def _sigmoid(x):
    return 0.5 * jnp.tanh(0.5 * x) + 0.5


def _rms(x, g):
    return x * lax.rsqrt(jnp.mean(x * x, axis=-1, keepdims=True) + EPS) * g


def _params(n_grid):
    return pltpu.CompilerParams(dimension_semantics=("arbitrary",) * n_grid,
                                vmem_limit_bytes=VMEM_LIMIT)


def _const_spec(shape, single_buffer=False):
    zeros = (0,) * len(shape)
    mode = pl.Buffered(1) if single_buffer else None
    return pl.BlockSpec(shape, lambda *_: zeros, pipeline_mode=mode)


def _mod_kernel(c_ref, w_ref, b_ref, o_ref):
    c = c_ref[...]
    s = (c * jax.nn.sigmoid(c)).astype(BF16)
    o_ref[0] = _dot(s, w_ref[0].astype(BF16)) + b_ref[0]


def _modulation(c_all, w_mod, b_mod):
    n_l, _, n_out = w_mod.shape
    rows = c_all.shape[0]
    bn = 512
    return pl.pallas_call(
        _mod_kernel,
        grid=(n_l, n_out // bn),
        in_specs=[pl.BlockSpec((rows, D_MODEL), lambda l, n: (0, 0)),
                  pl.BlockSpec((1, D_MODEL, bn), lambda l, n: (l, 0, n)),
                  pl.BlockSpec((1, 1, bn), lambda l, n: (l, 0, n))],
        out_specs=pl.BlockSpec((1, rows, bn), lambda l, n: (l, 0, n)),
        out_shape=jax.ShapeDtypeStruct((n_l, rows, n_out), F32),
        compiler_params=_params(2),
        name="modulation",
    )(c_all, w_mod, b_mod.reshape(n_l, 1, n_out))


def _pair_select(lane, shift):
    lane_up = pltpu.roll(lane, LANES - shift, 1)
    want = jnp.where((lane & (2 * shift - 1)) < shift, lane + shift, lane - shift)
    return lane_up == want


def _rope(x, cos, sin, shift, sel):
    partner = jnp.where(sel, pltpu.roll(x, LANES - shift, 1), pltpu.roll(x, shift, 1))
    return x * cos + partner * sin


def _group_sumsq(x, g):
    sq = x * x
    hi = sq.astype(BF16)
    lo = (sq - hi.astype(F32)).astype(BF16)
    return _dot(hi, g) + _dot(lo, g)


INPROJ_SUB_ROWS = 512


def _inproj_kernel(x_ref, mod_ref, wmix_ref, wuq_ref, wuk_ref, wuv_ref, pk_ref, g_ref,
                   gq_ref, gkv_ref, gdq_ref, gdk_ref,
                   cosd_ref, sind_ref, cosb_ref, sinb_ref, cosk_ref, sink_ref,
                   qa_o, ka_o, va_o, qb_o, kb_o, vb_o, xc_o, qd_o, kd_o, vd_o, vat_o, vbt_o, vdt_o):
    shift = mod_ref[0, 0:1, :]
    scale = mod_ref[0, 1:2, :]
    tile = x_ref.shape[1]
    sub = min(tile, INPROJ_SUB_ROWS)
    lane = lax.broadcasted_iota(jnp.int32, (sub, LANES), 1)
    sel16 = _pair_select(lane, 16)
    sel8 = _pair_select(lane, 8)
    for r0 in range(0, tile, sub):
        rows = pl.ds(r0, sub)
        xm = (_layer_norm(x_ref[0, rows, :]) * (1.0 + scale) + shift).astype(BF16)
        p = _dot(xm, wmix_ref[...])

        qa_o[0, rows, :] = (p[:, OFF_QA:OFF_QA + 256] * (NA_SCALE * LOG2E)).astype(BF16)
        ka_o[0, rows, :] = p[:, OFF_KA:OFF_KA + 256].astype(BF16)
        va = p[:, OFF_VA:OFF_VA + 256]
        va_o[0, rows, :] = va.astype(BF16)
        vat_o[0, :, rows] = va.T.astype(BF16)

        cqn = _rms(p[:, OFF_CQ:OFF_CQ + 256], gq_ref[...]).astype(BF16)
        qb = _dot(cqn, wuq_ref[...])
        cosb, sinb = cosb_ref[rows, :], sinb_ref[rows, :]
        for h in range(MLA_HEADS):
            blk = _rope(qb[:, LANES * h:LANES * (h + 1)], cosb, sinb, 8, sel8)
            qb_o[0, rows, LANES * h:LANES * (h + 1)] = (blk * (MLA_SCALE * LOG2E)).astype(BF16)
        ckvn = _rms(p[:, OFF_CKV:OFF_CKV + 128], gkv_ref[...]).astype(BF16)
        kr = _rope(p[:, OFF_KR:OFF_KR + 128], cosk_ref[rows, :], sink_ref[rows, :], 8, sel8).astype(BF16)
        kb_o[0, rows, :] = (_dot(ckvn, wuk_ref[...]) + _dot(kr, pk_ref[...])).astype(BF16)
        vb = _dot(ckvn, wuv_ref[...])
        vb_o[0, rows, :] = vb.astype(BF16)
        vbt_o[0, :, rows] = vb.T.astype(BF16)

        xc_o[0, rows, :] = p[:, OFF_LRU:OFF_LRU + 256]

        cosd, sind = cosd_ref[rows, :], sind_ref[rows, :]
        qd = p[:, OFF_QD:OFF_QD + 256]
        qd = qd * lax.rsqrt(_group_sumsq(qd, g_ref[...]) * (1.0 / HEAD_DIM) + EPS) * gdq_ref[...]
        for blk_i in range(2):
            blk = _rope(qd[:, LANES * blk_i:LANES * (blk_i + 1)], cosd, sind, 16, sel16)
            qd_o[0, rows, LANES * blk_i:LANES * (blk_i + 1)] = (blk * (GQA_SCALE * LOG2E)).astype(BF16)
        kd = p[:, OFF_KD:OFF_KD + 128]
        kd = kd * lax.rsqrt(_group_sumsq(kd, g_ref[0:128, 0:128]) * (1.0 / HEAD_DIM) + EPS) * gdk_ref[...]
        kd_o[0, rows, :] = _rope(kd, cosd, sind, 16, sel16).astype(BF16)
        vd = p[:, OFF_VD:OFF_VD + 128]
        vd_o[0, rows, :] = vd.astype(BF16)
        vdt_o[0, :, rows] = vd.T.astype(BF16)


def _inproj(x, mod, weights, tables, tile):
    b, n, _ = x.shape
    nt = n // tile
    wspecs = [_const_spec(w.shape) for w in weights]
    tspecs = [pl.BlockSpec((tile, LANES), lambda t, bb: (t, 0)) for _ in tables]
    widths = (256, 256, 256, 512, 512, 256, 256, 256, 128, 128)
    dtypes = (BF16, BF16, BF16, BF16, BF16, BF16, F32, BF16, BF16, BF16)
    t_rows = (256, 256, 128)
    return pl.pallas_call(
        _inproj_kernel,
        grid=(nt, b),
        in_specs=[pl.BlockSpec((1, tile, D_MODEL), lambda t, bb: (bb, t, 0)),
                  pl.BlockSpec((1, 3, D_MODEL), lambda t, bb: (bb, 0, 0))] + wspecs + tspecs,
        out_specs=[pl.BlockSpec((1, tile, w), lambda t, bb: (bb, t, 0)) for w in widths]
        + [pl.BlockSpec((1, r, tile), lambda t, bb: (bb, 0, t)) for r in t_rows],
        out_shape=[jax.ShapeDtypeStruct((b, n, w), dt) for w, dt in zip(widths, dtypes)]
        + [jax.ShapeDtypeStruct((b, r, n), BF16) for r in t_rows],
        compiler_params=_params(2),
        name="inproj",
    )(x, mod, *weights, *tables)


def _half_mask(lane, hh):
    return lane >= HEAD_DIM if hh else lane < HEAD_DIM


def _chunk_update(qh, kc, vc, m, l, acc):
    s = _dot_t(qh, kc)
    m_new = jnp.maximum(m, jnp.max(s, axis=-1, keepdims=True))
    alpha = jnp.exp2(m - m_new)
    pr = jnp.exp2(s - m_new)
    l = alpha * l + jnp.sum(pr, axis=-1, keepdims=True)
    acc = alpha * acc + _dot(pr.astype(BF16), vc)
    return m_new, l, acc


def _attn_kernel(*refs, masked, n_lat, chunk):
    if n_lat:
        q_ref, kl_ref, vl_ref, kc_ref, vc_ref, o_ref = refs
    else:
        q_ref, kc_ref, vc_ref, o_ref = refs
    tq = q_ref.shape[1]
    lane = lax.broadcasted_iota(jnp.int32, (tq, LANES), 1)
    qs, ksls = [], []
    for hh in range(2):
        if masked:
            qs.append(jnp.where(_half_mask(lane, hh), q_ref[0], jnp.zeros((), BF16)))
            ksls.append(slice(None))
        else:
            qs.append(q_ref[0, :, LANES * hh:LANES * (hh + 1)])
            ksls.append(slice(LANES * hh, LANES * (hh + 1)))
    carries = [(jnp.full((tq, 1), NEG_BIG, F32), jnp.zeros((tq, 1), F32), jnp.zeros((tq, LANES), F32))
               for _ in range(2)]
    for c0 in range(0, n_lat, chunk):
        rows = pl.ds(c0, chunk)
        for hh in range(2):
            carries[hh] = _chunk_update(qs[hh], kl_ref[0, rows, ksls[hh]], vl_ref[0, rows, :], *carries[hh])
    outs = []
    for hh in range(2):
        m, l, acc = _chunk_update(qs[hh], kc_ref[0, :, ksls[hh]], vc_ref[0], *carries[hh])
        outs.append(acc * (1.0 / l))
    o_ref[0] = jnp.where(lane < HEAD_DIM, outs[0], outs[1]).astype(BF16)


def _attention(q, k_lat, v_lat, k_ctx, v_ctx, *, mode, tq, chunk=1024):
    b, nq, _ = q.shape
    masked = mode != "B"
    qw = 256 if mode == "B" else LANES
    kw = 256 if mode == "B" else LANES
    kidx = (lambda bb, p, t: (bb, 0, 0)) if mode == "D" else (lambda bb, p, t: (bb, 0, p))
    n_ctx = k_ctx.shape[1]
    n_lat = 0 if k_lat is None else k_lat.shape[1]
    in_specs = [pl.BlockSpec((1, tq, qw), lambda bb, p, t: (bb, t, p))]
    args = [q]
    if n_lat:
        in_specs += [pl.BlockSpec((1, n_lat, kw), kidx), pl.BlockSpec((1, n_lat, LANES), kidx)]
        args += [k_lat, v_lat]
    in_specs += [pl.BlockSpec((1, n_ctx, kw), kidx), pl.BlockSpec((1, n_ctx, LANES), kidx)]
    args += [k_ctx, v_ctx]
    return pl.pallas_call(
        functools.partial(_attn_kernel, masked=masked, n_lat=n_lat, chunk=chunk),
        grid=(b, 2, nq // tq),
        in_specs=in_specs,
        out_specs=pl.BlockSpec((1, tq, LANES), lambda bb, p, t: (bb, t, p)),
        out_shape=jax.ShapeDtypeStruct((b, nq, 256), BF16),
        compiler_params=_params(3),
        name="attn_" + mode + ("_lat" if n_lat else "_ctx"),
    )(*args)


def _with_ones_row(vt, hh):
    row = lax.broadcasted_iota(jnp.int32, vt.shape, 0)
    return jnp.where(row == _ones_row(hh), jnp.ones((), BF16), vt)


def _ones_row(hh):
    return 0 if hh else HEAD_DIM


def _softmax_update_t(s, vtc, m, acc):
    m_new = jnp.maximum(m, jnp.max(s, axis=0, keepdims=True))
    alpha = jnp.exp2(m - m_new)
    pr = jnp.exp2(s - m_new)
    acc = alpha * acc + _dot(vtc, pr.astype(BF16))
    return m_new, acc


def _finish_t(accs):
    outs = [acc * (1.0 / acc[_ones_row(hh):_ones_row(hh) + 1, :]) for hh, acc in enumerate(accs)]
    row = lax.broadcasted_iota(jnp.int32, outs[0].shape, 0)
    return jnp.where(row < HEAD_DIM, outs[0], outs[1]).T.astype(BF16)


ATTN_AHEAD = 2


def _attn_t_kernel(q_ref, kl_ref, vtl_ref, kc_ref, vtc_ref, o_ref, *, masked, chunk):
    tq = q_ref.shape[1]
    n_lat = kl_ref.shape[1]
    lane = lax.broadcasted_iota(jnp.int32, (tq, LANES), 1)
    qs, ksls = [], []
    for hh in range(2):
        if masked:
            qs.append(jnp.where(_half_mask(lane, hh), q_ref[0], jnp.zeros((), BF16)))
            ksls.append(slice(None))
        else:
            qs.append(q_ref[0, :, LANES * hh:LANES * (hh + 1)])
            ksls.append(slice(LANES * hh, LANES * (hh + 1)))
    carries = [(jnp.full((1, tq), NEG_BIG, F32), jnp.zeros((LANES, tq), F32)) for _ in range(2)]
    tasks = [(hh, (kl_ref, pl.ds(c0, chunk)), (vtl_ref, pl.ds(c0, chunk)))
             for c0 in range(0, n_lat, chunk) for hh in range(2)]
    tasks += [(hh, (kc_ref, slice(None)), (vtc_ref, slice(None))) for hh in range(2)]

    def scores(task):
        hh, (k_ref, rows), _ = task
        return _dot_t(k_ref[0, rows, ksls[hh]], qs[hh])

    ahead = [scores(task) for task in tasks[:ATTN_AHEAD]]
    for i, task in enumerate(tasks):
        s_cur = ahead.pop(0)
        if i + ATTN_AHEAD < len(tasks):
            ahead.append(scores(tasks[i + ATTN_AHEAD]))
        hh, _, (vt_ref, cols) = task
        carries[hh] = _softmax_update_t(s_cur, _with_ones_row(vt_ref[0, :, cols], hh), *carries[hh])
    o_ref[0] = _finish_t([acc for _, acc in carries])


def _attention_t(q, k_lat, vt_lat, k_ctx, vt_ctx, *, mode, tq, chunk=1024):
    b, nq, _ = q.shape
    masked = mode != "B"
    qw = 256 if mode == "B" else LANES
    kidx = (lambda bb, p, t: (bb, 0, 0)) if mode == "D" else (lambda bb, p, t: (bb, 0, p))
    vidx = (lambda bb, p, t: (bb, 0, 0)) if mode == "D" else (lambda bb, p, t: (bb, p, 0))
    n_lat, n_ctx = k_lat.shape[1], k_ctx.shape[1]
    return pl.pallas_call(
        functools.partial(_attn_t_kernel, masked=masked, chunk=chunk),
        grid=(b, 2, nq // tq),
        in_specs=[pl.BlockSpec((1, tq, qw), lambda bb, p, t: (bb, t, p)),
                  pl.BlockSpec((1, n_lat, qw), kidx), pl.BlockSpec((1, LANES, n_lat), vidx),
                  pl.BlockSpec((1, n_ctx, qw), kidx), pl.BlockSpec((1, LANES, n_ctx), vidx)],
        out_specs=pl.BlockSpec((1, tq, LANES), lambda bb, p, t: (bb, t, p)),
        out_shape=jax.ShapeDtypeStruct((b, nq, 256), BF16),
        compiler_params=_params(3),
        name="attn_" + mode + "_lat",
    )(q, k_lat, vt_lat, k_ctx, vt_ctx)


NA_QROWS = 8
NA_KROWS = 16
NA_TQ = NA_QROWS * GRID_W
NA_KBLK = 256


def _na_kernel(q_ref, k0, k1, k2, k3, vt0, vt1, vt2, vt3, kc_ref, vtc_ref, bias_ref, o_ref):
    q = q_ref[0]
    lane = lax.broadcasted_iota(jnp.int32, (NA_TQ, LANES), 1)
    k_win = jnp.concatenate([k0[0], k1[0], k2[0], k3[0]], axis=0)
    vt_win = jnp.concatenate([vt0[0], vt1[0], vt2[0], vt3[0]], axis=1)
    qs = [jnp.where(_half_mask(lane, hh), q, jnp.zeros((), BF16)) for hh in range(2)]
    scores = [(_dot_t(k_win, qh) + bias_ref[hh, 0], _dot_t(kc_ref[0], qh)) for hh, qh in enumerate(qs)]
    accs = []
    for hh, (s_win, s_ctx) in enumerate(scores):
        m = jnp.maximum(jnp.max(s_win, axis=0, keepdims=True), jnp.max(s_ctx, axis=0, keepdims=True))
        p_win = jnp.exp2(s_win - m).astype(BF16)
        p_ctx = jnp.exp2(s_ctx - m).astype(BF16)
        accs.append(_dot(_with_ones_row(vt_win, hh), p_win) + _dot(_with_ones_row(vtc_ref[0], hh), p_ctx))
    o_ref[0] = _finish_t(accs)


def _na_bias(rel_bias):
    h, n_dr, n_dc = rel_bias.shape
    n_rows = 64
    w2 = 2 * GRID_W
    pad_lo = GRID_W - WIN_W
    p = jnp.pad(rel_bias * LOG2E, ((0, 0), (0, 0), (pad_lo, w2 - pad_lo - n_dc)))
    skew = jnp.broadcast_to(p[:, :, None, :], (h, n_dr, GRID_W, w2)).reshape(h, n_dr, GRID_W * w2)
    skew = skew[:, :, :GRID_W * (w2 - 1)].reshape(h, n_dr, GRID_W, w2 - 1)
    tiles = skew[..., GRID_W - 1:]
    qc = np.arange(GRID_W)
    cs = np.clip(qc - WIN_W // 2, 0, GRID_W - WIN_W)
    in_col = (qc[None, :] >= cs[:, None]) & (qc[None, :] < cs[:, None] + WIN_W)
    tiles = jnp.where(in_col, tiles, NEG_BIG)
    tiles = jnp.concatenate([tiles, jnp.full((h, 1, GRID_W, GRID_W), NEG_BIG, F32)], axis=1)
    ids = np.full((3, NA_KROWS, NA_QROWS), n_dr, np.int32)
    for v, j in enumerate((0, 1, n_rows // NA_QROWS - 1)):
        kb = int(np.clip(NA_QROWS * j - WIN_H // 2, 0, n_rows - NA_KROWS))
        for i in range(NA_QROWS):
            qr = NA_QROWS * j + i
            rs = int(np.clip(qr - WIN_H // 2, 0, n_rows - WIN_H))
            for kr in range(rs, rs + WIN_H):
                ids[v, kr - kb, i] = kr - qr + WIN_H - 1
    blocks = tiles[:, ids]
    return blocks.transpose(0, 1, 2, 5, 3, 4).reshape(h, 3, NA_KROWS * GRID_W, NA_TQ)


def _na_attention(q, k, vt, k_ctx, vt_ctx, bias):
    b, n, _ = q.shape
    nj = n // NA_TQ
    n_kblk = n // NA_KBLK
    n_ctx = k_ctx.shape[1]

    def kstart(j):
        return jnp.clip(2 * j - 1, 0, n_kblk - 4)

    def kspec(i):
        return pl.BlockSpec((1, NA_KBLK, LANES), lambda p, j, bb: (bb, kstart(j) + i, p))

    def vspec(i):
        return pl.BlockSpec((1, LANES, NA_KBLK), lambda p, j, bb: (bb, p, kstart(j) + i))

    def variant(j):
        return jnp.where(j == 0, 0, jnp.where(j == nj - 1, 2, 1))

    return pl.pallas_call(
        _na_kernel,
        grid=(2, nj, b),
        in_specs=[pl.BlockSpec((1, NA_TQ, LANES), lambda p, j, bb: (bb, j, p))]
        + [kspec(i) for i in range(4)] + [vspec(i) for i in range(4)]
        + [pl.BlockSpec((1, n_ctx, LANES), lambda p, j, bb: (bb, 0, p)),
           pl.BlockSpec((1, LANES, n_ctx), lambda p, j, bb: (bb, p, 0)),
           pl.BlockSpec((2, 1, NA_KROWS * GRID_W, NA_TQ), lambda p, j, bb: (p, variant(j), 0, 0))],
        out_specs=pl.BlockSpec((1, NA_TQ, LANES), lambda p, j, bb: (bb, j, p)),
        out_shape=jax.ShapeDtypeStruct((b, n, 256), BF16),
        compiler_params=_params(3),
        name="na_lat",
    )(q, k, k, k, k, vt, vt, vt, vt, k_ctx, vt_ctx, bias)


LRU_SEG = 8
LRU_PAD = 8
LRU_ROWS = 512
LRU_HALVES = 2


def _pitch(n):
    seg = n // LRU_SEG
    return seg + 8 if (seg // 8) % 2 == 0 else seg


def _pieces(n, t0, r):
    seg, out, t = n // LRU_SEG, [], t0
    while t < t0 + r:
        length = min(seg - t % seg, t0 + r - t)
        out.append((t - t0, (t // seg) * _pitch(n) + t % seg, length))
        t += length
    return out


def _put(ref, n, t0, val):
    for off, row, length in _pieces(n, t0, val.shape[0]):
        for hv in range(LRU_HALVES):
            ref[hv, pl.ds(row, length), :] = val[off:off + length, LANES * hv:LANES * (hv + 1)]


def _get(ref, n, t0, r):
    return jnp.concatenate(
        [jnp.concatenate([ref[hv, pl.ds(row, length), :] for hv in range(LRU_HALVES)], axis=-1)
         for _, row, length in _pieces(n, t0, r)], axis=0)


def _lru_gates(xp, n, cw, cb, wa_ref, ba_ref, wx_ref, bx_ref, sp, a_refs, u_refs):
    r = min(LRU_ROWS, n)
    for c0 in range(0, n, r):
        conv = cb
        for tap in range(CONV_W):
            conv = conv + cw[tap:tap + 1, :] * xp[pl.ds(LRU_PAD + c0 + tap - CONV_W // 2, r), :]
        xb = conv.astype(BF16)
        for d in range(2):
            rg = _sigmoid(_dot(xb, wa_ref[d]) + ba_ref[d:d + 1, :])
            ig = _sigmoid(_dot(xb, wx_ref[d]) + bx_ref[d:d + 1, :])
            a = jnp.exp((-LRU_C) * rg * sp[d:d + 1, :])
            _put(a_refs[d], n, c0, a)
            _put(u_refs[d], n, c0, jnp.sqrt(1.0 - a * a) * (ig * conv))


def _lru_scan(n, af, uf, ab, ub, hf0, hb0):
    seg, pitch = n // LRU_SEG, _pitch(n)

    def body(i, carry):
        rf = pl.ds(i, LRU_SEG, stride=pitch)
        rb = pl.ds(seg - 1 - i, LRU_SEG, stride=pitch)
        out = []
        for hv in range(LRU_HALVES):
            hlf, cmf, hlb, cmb = carry[4 * hv:4 * hv + 4]
            a1, u1 = af[hv, rf, :], uf[hv, rf, :]
            a2, u2 = ab[hv, rb, :], ub[hv, rb, :]
            hlf = a1 * hlf + u1
            cmf = a1 * cmf
            hlb = a2 * hlb + u2
            cmb = a2 * cmb
            uf[hv, rf, :] = hlf
            af[hv, rf, :] = cmf
            ub[hv, rb, :] = hlb
            ab[hv, rb, :] = cmb
            out += [hlf, cmf, hlb, cmb]
        return tuple(out)

    zero = jnp.zeros((LRU_SEG, LANES), F32)
    one = jnp.ones((LRU_SEG, LANES), F32)
    lax.fori_loop(0, seg, body, (zero, one, zero, one) * LRU_HALVES)
    hf, hb = list(hf0), list(hb0)
    for s in range(LRU_SEG):
        rf = pl.ds(s * pitch, seg)
        rb = pl.ds((LRU_SEG - 1 - s) * pitch, seg)
        for hv in range(LRU_HALVES):
            h = uf[hv, rf, :] + af[hv, rf, :] * hf[hv]
            uf[hv, rf, :] = h
            hf[hv] = h[seg - 1:seg, :]
            h = ub[hv, rb, :] + ab[hv, rb, :] * hb[hv]
            ub[hv, rb, :] = h
            hb[hv] = h[0:1, :]
    return hf, hb


def _lru_kernel(xl_ref, xc_ref, cw_ref, cb_ref, wa_ref, ba_ref, wx_ref, bx_ref, lam_ref,
                yl_ref, yc_ref, xp, afl, ufl, abl, ubl, afc, ufc, abc, ubc):
    n_lat, w = xl_ref.shape[1], xl_ref.shape[2]
    n_ctx = xc_ref.shape[1]
    cw, cb = cw_ref[...], cb_ref[...]
    neg_lam = -lam_ref[...]
    sp = jnp.maximum(neg_lam, 0.0) + jnp.log1p(jnp.exp(-jnp.abs(neg_lam)))
    zpad = jnp.zeros((LRU_PAD, w), F32)

    xp[pl.ds(0, LRU_PAD), :] = zpad
    xp[pl.ds(LRU_PAD, n_ctx), :] = xc_ref[0]
    xp[pl.ds(LRU_PAD + n_ctx, LRU_PAD), :] = zpad
    _lru_gates(xp, n_ctx, cw, cb, wa_ref, ba_ref, wx_ref, bx_ref, sp, (afc, abc), (ufc, ubc))
    xp[pl.ds(LRU_PAD, n_lat), :] = xl_ref[0]
    xp[pl.ds(LRU_PAD + n_lat, LRU_PAD), :] = zpad
    _lru_gates(xp, n_lat, cw, cb, wa_ref, ba_ref, wx_ref, bx_ref, sp, (afl, abl), (ufl, ubl))

    h0 = [jnp.zeros((1, LANES), F32)] * LRU_HALVES
    hf, hb = _lru_scan(n_ctx, afc, ufc, abc, ubc, h0, h0)
    _lru_scan(n_lat, afl, ufl, abl, ubl, hf, hb)
    yc_ref[0] = (_get(ufc, n_ctx, 0, n_ctx) + _get(ubc, n_ctx, 0, n_ctx)).astype(BF16)
    for c0 in range(0, n_lat, LRU_ROWS):
        yl_ref[0, pl.ds(c0, LRU_ROWS), :] = (_get(ufl, n_lat, c0, LRU_ROWS)
                                             + _get(ubl, n_lat, c0, LRU_ROWS)).astype(BF16)


def _lru(x_lat, x_ctx, conv_w, conv_b, wa, ba, wx, bx, lam):
    b, n_lat, w = x_lat.shape
    n_ctx = x_ctx.shape[1]
    consts = (conv_w, conv_b, wa, ba, wx, bx, lam)
    big = lambda: pltpu.VMEM((LRU_HALVES, LRU_SEG * _pitch(n_lat), LANES), F32)
    small = lambda: pltpu.VMEM((LRU_HALVES, LRU_SEG * _pitch(n_ctx), LANES), F32)
    return pl.pallas_call(
        _lru_kernel,
        grid=(b,),
        in_specs=[pl.BlockSpec((1, n_lat, w), lambda bb: (bb, 0, 0)),
                  pl.BlockSpec((1, n_ctx, w), lambda bb: (bb, 0, 0))]
        + [_const_spec(c.shape) for c in consts],
        out_specs=[pl.BlockSpec((1, n_lat, w), lambda bb: (bb, 0, 0)),
                   pl.BlockSpec((1, n_ctx, w), lambda bb: (bb, 0, 0))],
        out_shape=[jax.ShapeDtypeStruct((b, n_lat, w), BF16),
                   jax.ShapeDtypeStruct((b, n_ctx, w), BF16)],
        scratch_shapes=[pltpu.VMEM((n_lat + 2 * LRU_PAD, w), F32),
                        big(), big(), big(), big(), small(), small(), small(), small()],
        compiler_params=_params(1),
        name="rglru",
    )(x_lat, x_ctx, *consts)


MERGE_SUBTILES = 2


def _merge_kernel(x_ref, mod_ref, ya_ref, yb_ref, yc_ref, yd_ref, wz_ref, wm_ref, wb_ref, wo_ref,
                  lng_ref, lnb_ref, o_ref):
    shift = mod_ref[0, 0:1, :]
    scale = mod_ref[0, 1:2, :]
    gate = mod_ref[0, 2:3, :]
    tile = x_ref.shape[1]
    sub = tile // MERGE_SUBTILES
    for r0 in range(0, tile, sub):
        rows = pl.ds(r0, sub)
        x = x_ref[0, rows, :]
        xm = (_layer_norm(x) * (1.0 + scale) + shift).astype(BF16)
        z = _dot(xm, wz_ref[...])
        acc = None
        for i, y_ref in enumerate((ya_ref, yb_ref, yc_ref, yd_ref)):
            zi = z[:, BRANCH_W * i:BRANCH_W * (i + 1)]
            g = (y_ref[0, rows, :].astype(F32) * (zi * _sigmoid(zi))).astype(BF16)
            t = _dot(g, wb_ref[i])
            mi = _dot(xm, wm_ref[:, D_MODEL * i:D_MODEL * (i + 1)])
            term = _sigmoid(mi) * t
            acc = term if acc is None else acc + term
        out = _dot(acc.astype(BF16), wo_ref[...])
        o_ref[0, rows, :] = _layer_norm(DEEPNORM_ALPHA * x + gate * out) * lng_ref[...] + lnb_ref[...]


def _merge(x, mod, ys, wz, wm, wb, wo, ln_g, ln_b, tile):
    b, n, _ = x.shape
    consts = (wz, wm, wb, wo, ln_g, ln_b)
    tok = lambda w: pl.BlockSpec((1, tile, w), lambda bb, t: (bb, t, 0))
    return pl.pallas_call(
        _merge_kernel,
        grid=(b, n // tile),
        in_specs=[tok(D_MODEL), pl.BlockSpec((1, 3, D_MODEL), lambda bb, t: (bb, 0, 0))]
        + [tok(BRANCH_W) for _ in ys] + [_const_spec(c.shape, single_buffer=True) for c in consts],
        out_specs=tok(D_MODEL),
        out_shape=jax.ShapeDtypeStruct((b, n, D_MODEL), F32),
        compiler_params=_params(2),
        name="merge",
    )(x, mod, *ys, *consts)


def _rope_tables(n):
    t = jnp.arange(n, dtype=jnp.int32)
    rows = (t // GRID_W).astype(F32)[:, None]
    cols = (t % GRID_W).astype(F32)[:, None]

    def table(half, offsets):
        inv = ROPE_THETA ** (-np.arange(0, 2 * half, 2, dtype=np.float64) / (2 * half))
        f_row, f_col, sign = (np.zeros(LANES, np.float32) for _ in range(3))
        for offset in offsets:
            for g, f in enumerate((f_row, f_col)):
                base = offset + 2 * half * g
                f[base:base + 2 * half] = np.concatenate([inv, inv])
                sign[base:base + 2 * half] = np.concatenate([-np.ones(half), np.ones(half)])
        ang = rows * f_row[None, :] + cols * f_col[None, :]
        return jnp.cos(ang), jnp.sin(ang) * sign[None, :]

    cosd, sind = table(16, (0, HEAD_DIM))
    cosb, sinb = table(8, (MLA_NOPE,))
    cosk, sink = table(8, (0,))
    return cosd, sind, cosb, sinb, cosk, sink


def _identity_tables(n):
    one, zero = jnp.ones((n, LANES), F32), jnp.zeros((n, LANES), F32)
    return one, zero, one, zero, one, zero


def _layer_weights(w_in, mla_q_norm, mla_w_uq, mla_kv_norm, mla_w_ukv, gqa_q_norm, gqa_k_norm, w_branch, w_out):
    wi = w_in[:, :MIX_COLS]
    qa, ka, va = wi[:, 0:256], wi[:, 256:512], wi[:, 512:768]
    cq, ckv, kr = wi[:, 768:1024], wi[:, 1024:1152], wi[:, 1152:1184]
    lru, qd, kd, vd = wi[:, 1184:1440], wi[:, 1440:1696], wi[:, 1696:1824], wi[:, 1824:1952]
    head_order = jnp.array([0, 2, 1, 3])
    qd = qd.reshape(D_MODEL, 4, HEAD_DIM)[:, head_order].reshape(D_MODEL, 256)
    wmix = jnp.concatenate([qa, ka, va, cq, ckv, lru, qd, kd, vd, kr,
                            jnp.zeros((D_MODEL, MIX_PAD - MIX_COLS), F32)], -1).astype(BF16)
    hq = MLA_NOPE + MLA_ROPE
    wuq = jnp.pad(mla_w_uq.reshape(256, MLA_HEADS, hq), ((0, 0), (0, 0), (0, LANES - hq)))
    wuq = wuq.reshape(256, MLA_HEADS * LANES).astype(BF16)
    ukv = mla_w_ukv.reshape(128, MLA_HEADS, 128)
    wuk = jnp.pad(ukv[:, :, :MLA_NOPE], ((0, 0), (0, 0), (0, LANES - MLA_NOPE)))
    wuk = wuk.reshape(128, MLA_HEADS * LANES).astype(BF16)
    wuv = ukv[:, :, MLA_NOPE:].reshape(128, 256).astype(BF16)
    pk = np.zeros((LANES, MLA_HEADS * LANES), np.float32)
    for h in range(MLA_HEADS):
        pk[np.arange(MLA_ROPE), LANES * h + MLA_NOPE + np.arange(MLA_ROPE)] = 1.0
    grp = np.kron(np.eye(4, dtype=np.float32), np.ones((HEAD_DIM, HEAD_DIM), np.float32))
    inproj_w = (wmix, wuq, wuk, wuv, jnp.asarray(pk, BF16), jnp.asarray(grp, BF16),
                mla_q_norm.reshape(1, 256), mla_kv_norm.reshape(1, 128),
                jnp.tile(gqa_q_norm, 4).reshape(1, 256), jnp.tile(gqa_k_norm, 2).reshape(1, 128))
    wz = w_in[:, MIX_COLS:MIX_COLS + SILU_COLS]
    wz_d = wz[:, 3 * BRANCH_W:].reshape(D_MODEL, 4, HEAD_DIM)[:, head_order].reshape(D_MODEL, BRANCH_W)
    wz = jnp.concatenate([wz[:, :3 * BRANCH_W], wz_d], -1).astype(BF16)
    wm = w_in[:, MIX_COLS + SILU_COLS:].astype(BF16)
    wb_d = w_branch[3].reshape(4, HEAD_DIM, D_MODEL)[head_order].reshape(BRANCH_W, D_MODEL)
    wb = jnp.concatenate([w_branch[:3], wb_d[None]], 0).astype(BF16)
    return inproj_w, (wz, wm, wb, w_out.astype(BF16))


def _block_diag(w):
    eye = jnp.eye(4, dtype=w.dtype)
    return jnp.einsum("dkce,kj->dkcje", w, eye).reshape(2, 256, 256)


def kernel(x, c, ctx, c_ctx, w_mod, b_mod, w_in, na_rel_bias, mla_q_norm, mla_w_uq, mla_kv_norm, mla_w_ukv,
           lru_conv_w, lru_conv_b, lru_w_a, lru_b_a, lru_w_x, lru_b_x, lru_lambda, gqa_q_norm, gqa_k_norm,
           w_branch, w_out, ln_g, ln_b):
    b, n, _ = x.shape
    n_ctx = ctx.shape[1]
    rows = 8 * ((b + 1 + 7) // 8)
    c_all = jnp.zeros((rows, D_MODEL), F32).at[:b].set(c).at[b].set(c_ctx)
    mod_all = _modulation(c_all, w_mod, b_mod)
    lat_tables = _rope_tables(n)
    ctx_tables = _identity_tables(n_ctx)

    for l in range(DEPTH):
        need_ctx = l < DEPTH - 1
        mod_lat = mod_all[l, :b].reshape(b, 3, D_MODEL)
        mod_ctx = jnp.broadcast_to(mod_all[l, b].reshape(1, 3, D_MODEL), (b, 3, D_MODEL))
        inproj_w, merge_w = _layer_weights(w_in[l], mla_q_norm[l], mla_w_uq[l], mla_kv_norm[l], mla_w_ukv[l],
                                           gqa_q_norm[l], gqa_k_norm[l], w_branch[l], w_out[l])
        qa, ka, va, qb, kb, vb, xr, qd, kd, vd, vat, vbt, vdt = _inproj(x, mod_lat, inproj_w, lat_tables, 1024)
        qa_c, ka_c, va_c, qb_c, kb_c, vb_c, xr_c, qd_c, kd_c, vd_c, vat_c, vbt_c, vdt_c = _inproj(
            ctx, mod_ctx, inproj_w, ctx_tables, n_ctx)

        ya = _na_attention(qa, ka, vat, ka_c, vat_c, _na_bias(na_rel_bias[l]))
        yb = _attention_t(qb, kb, vbt, kb_c, vbt_c, mode="B", tq=1024)
        yd = _attention_t(qd, kd, vdt, kd_c, vdt_c, mode="D", tq=1024)
        yc, yc_c = _lru(xr, xr_c, lru_conv_w[l], lru_conv_b[l].reshape(1, -1),
                        _block_diag(lru_w_a[l]).astype(BF16), lru_b_a[l],
                        _block_diag(lru_w_x[l]).astype(BF16), lru_b_x[l], lru_lambda[l])
        tail = (*merge_w, ln_g[l].reshape(1, -1), ln_b[l].reshape(1, -1))
        x_new = _merge(x, mod_lat, (ya, yb, yc, yd), *tail, 1024)
        if need_ctx:
            ya_c = _attention(qa_c, None, None, ka_c, va_c, mode="A", tq=n_ctx)
            yb_c = _attention(qb_c, None, None, kb_c, vb_c, mode="B", tq=n_ctx)
            yd_c = _attention(qd_c, None, None, kd_c, vd_c, mode="D", tq=n_ctx)
            ctx = _merge(ctx, mod_ctx, (ya_c, yb_c, yc_c, yd_c), *tail, n_ctx)
        x = x_new
    return x
```

```python
import functools
import math

import numpy as np
import jax
import jax.numpy as jnp
from jax import lax
from jax.experimental import pallas as pl
from jax.experimental.pallas import tpu as pltpu

F32 = jnp.float32
BF16 = jnp.bfloat16

D_MODEL = 1024
DEPTH = 2
GRID_W = 64
BRANCH_W = 256
HEAD_DIM = 64
WIN_H = 8
WIN_W = 16
MLA_HEADS = 4
MLA_NOPE = 64
MLA_ROPE = 32
LRU_C = 8.0
CONV_W = 4
ROPE_THETA = 10000.0
EPS = 1e-6
LOG2E = 1.4426950408889634
NA_SCALE = HEAD_DIM ** -0.5
MLA_SCALE = (MLA_NOPE + MLA_ROPE) ** -0.5
GQA_SCALE = HEAD_DIM ** -0.5
DEEPNORM_ALPHA = (2 * DEPTH) ** 0.25

MIX_COLS = 1952
SILU_COLS = 1024
LANES = 128
NEG_BIG = -1e30
VMEM_LIMIT = 56 * 1024 * 1024

OFF_QA, OFF_KA, OFF_VA, OFF_CQ, OFF_CKV, OFF_LRU, OFF_QD, OFF_KD, OFF_VD, OFF_KR = (
    0, 256, 512, 768, 1024, 1152, 1408, 1664, 1792, 1920)
MIX_PAD = 2048


def _dot(a, b):
    return jnp.dot(a, b, preferred_element_type=F32)


def _dot_t(a, b):
    return lax.dot_general(a, b, (((1,), (1,)), ((), ())), preferred_element_type=F32)


def _layer_norm(x):
    mu = jnp.mean(x, axis=-1, keepdims=True)
    xc = x - mu
    var = jnp.mean(xc * xc, axis=-1, keepdims=True)
    return xc * lax.rsqrt(var + EPS)


def _sigmoid(x):
    return 0.5 * jnp.tanh(0.5 * x) + 0.5


def _rms(x, g):
    return x * lax.rsqrt(jnp.mean(x * x, axis=-1, keepdims=True) + EPS) * g


def _params(n_grid):
    return pltpu.CompilerParams(dimension_semantics=("arbitrary",) * n_grid,
                                vmem_limit_bytes=VMEM_LIMIT)


def _const_spec(shape, single_buffer=False):
    zeros = (0,) * len(shape)
    mode = pl.Buffered(1) if single_buffer else None
    return pl.BlockSpec(shape, lambda *_: zeros, pipeline_mode=mode)


def _mod_kernel(c_ref, w_ref, b_ref, o_ref):
    c = c_ref[...]
    s = (c * jax.nn.sigmoid(c)).astype(BF16)
    o_ref[0] = _dot(s, w_ref[0].astype(BF16)) + b_ref[0]


def _modulation(c_all, w_mod, b_mod):
    n_l, _, n_out = w_mod.shape
    rows = c_all.shape[0]
    bn = 512
    return pl.pallas_call(
        _mod_kernel,
        grid=(n_l, n_out // bn),
        in_specs=[pl.BlockSpec((rows, D_MODEL), lambda l, n: (0, 0)),
                  pl.BlockSpec((1, D_MODEL, bn), lambda l, n: (l, 0, n)),
                  pl.BlockSpec((1, 1, bn), lambda l, n: (l, 0, n))],
        out_specs=pl.BlockSpec((1, rows, bn), lambda l, n: (l, 0, n)),
        out_shape=jax.ShapeDtypeStruct((n_l, rows, n_out), F32),
        compiler_params=_params(2),
        name="modulation",
    )(c_all, w_mod, b_mod.reshape(n_l, 1, n_out))


def _pair_select(lane, shift):
    lane_up = pltpu.roll(lane, LANES - shift, 1)
    want = jnp.where((lane & (2 * shift - 1)) < shift, lane + shift, lane - shift)
    return lane_up == want


def _rope(x, cos, sin, shift, sel):
    partner = jnp.where(sel, pltpu.roll(x, LANES - shift, 1), pltpu.roll(x, shift, 1))
    return x * cos + partner * sin


def _group_sumsq(x, g):
    sq = x * x
    hi = sq.astype(BF16)
    lo = (sq - hi.astype(F32)).astype(BF16)
    return _dot(hi, g) + _dot(lo, g)


INPROJ_SUB_ROWS = 512


def _inproj_kernel(x_ref, mod_ref, wmix_ref, wuq_ref, wuk_ref, wuv_ref, pk_ref, g_ref,
                   gq_ref, gkv_ref, gdq_ref, gdk_ref,
                   cosd_ref, sind_ref, cosb_ref, sinb_ref, cosk_ref, sink_ref,
                   qa_o, ka_o, va_o, qb_o, kb_o, vb_o, xc_o, qd_o, kd_o, vd_o, vat_o, vbt_o, vdt_o):
    shift = mod_ref[0, 0:1, :]
    scale = mod_ref[0, 1:2, :]
    tile = x_ref.shape[1]
    sub = min(tile, INPROJ_SUB_ROWS)
    lane = lax.broadcasted_iota(jnp.int32, (sub, LANES), 1)
    sel16 = _pair_select(lane, 16)
    sel8 = _pair_select(lane, 8)
    for r0 in range(0, tile, sub):
        rows = pl.ds(r0, sub)
        xm = (_layer_norm(x_ref[0, rows, :]) * (1.0 + scale) + shift).astype(BF16)
        p = _dot(xm, wmix_ref[...])

        qa_o[0, rows, :] = (p[:, OFF_QA:OFF_QA + 256] * (NA_SCALE * LOG2E)).astype(BF16)
        ka_o[0, rows, :] = p[:, OFF_KA:OFF_KA + 256].astype(BF16)
        va = p[:, OFF_VA:OFF_VA + 256]
        va_o[0, rows, :] = va.astype(BF16)
        vat_o[0, :, rows] = va.T.astype(BF16)

        cqn = _rms(p[:, OFF_CQ:OFF_CQ + 256], gq_ref[...]).astype(BF16)
        qb = _dot(cqn, wuq_ref[...])
        cosb, sinb = cosb_ref[rows, :], sinb_ref[rows, :]
        for h in range(MLA_HEADS):
            blk = _rope(qb[:, LANES * h:LANES * (h + 1)], cosb, sinb, 8, sel8)
            qb_o[0, rows, LANES * h:LANES * (h + 1)] = (blk * (MLA_SCALE * LOG2E)).astype(BF16)
        ckvn = _rms(p[:, OFF_CKV:OFF_CKV + 128], gkv_ref[...]).astype(BF16)
        kr = _rope(p[:, OFF_KR:OFF_KR + 128], cosk_ref[rows, :], sink_ref[rows, :], 8, sel8).astype(BF16)
        kb_o[0, rows, :] = (_dot(ckvn, wuk_ref[...]) + _dot(kr, pk_ref[...])).astype(BF16)
        vb = _dot(ckvn, wuv_ref[...])
        vb_o[0, rows, :] = vb.astype(BF16)
        vbt_o[0, :, rows] = vb.T.astype(BF16)

        xc_o[0, rows, :] = p[:, OFF_LRU:OFF_LRU + 256]

        cosd, sind = cosd_ref[rows, :], sind_ref[rows, :]
        qd = p[:, OFF_QD:OFF_QD + 256]
        qd = qd * lax.rsqrt(_group_sumsq(qd, g_ref[...]) * (1.0 / HEAD_DIM) + EPS) * gdq_ref[...]
        for blk_i in range(2):
            blk = _rope(qd[:, LANES * blk_i:LANES * (blk_i + 1)], cosd, sind, 16, sel16)
            qd_o[0, rows, LANES * blk_i:LANES * (blk_i + 1)] = (blk * (GQA_SCALE * LOG2E)).astype(BF16)
        kd = p[:, OFF_KD:OFF_KD + 128]
        kd = kd * lax.rsqrt(_group_sumsq(kd, g_ref[0:128, 0:128]) * (1.0 / HEAD_DIM) + EPS) * gdk_ref[...]
        kd_o[0, rows, :] = _rope(kd, cosd, sind, 16, sel16).astype(BF16)
        vd = p[:, OFF_VD:OFF_VD + 128]
        vd_o[0, rows, :] = vd.astype(BF16)
        vdt_o[0, :, rows] = vd.T.astype(BF16)


def _inproj(x, mod, weights, tables, tile):
    b, n, _ = x.shape
    nt = n // tile
    wspecs = [_const_spec(w.shape) for w in weights]
    tspecs = [pl.BlockSpec((tile, LANES), lambda t, bb: (t, 0)) for _ in tables]
    widths = (256, 256, 256, 512, 512, 256, 256, 256, 128, 128)
    dtypes = (BF16, BF16, BF16, BF16, BF16, BF16, F32, BF16, BF16, BF16)
    t_rows = (256, 256, 128)
    return pl.pallas_call(
        _inproj_kernel,
        grid=(nt, b),
        in_specs=[pl.BlockSpec((1, tile, D_MODEL), lambda t, bb: (bb, t, 0)),
                  pl.BlockSpec((1, 3, D_MODEL), lambda t, bb: (bb, 0, 0))] + wspecs + tspecs,
        out_specs=[pl.BlockSpec((1, tile, w), lambda t, bb: (bb, t, 0)) for w in widths]
        + [pl.BlockSpec((1, r, tile), lambda t, bb: (bb, 0, t)) for r in t_rows],
        out_shape=[jax.ShapeDtypeStruct((b, n, w), dt) for w, dt in zip(widths, dtypes)]
        + [jax.ShapeDtypeStruct((b, r, n), BF16) for r in t_rows],
        compiler_params=_params(2),
        name="inproj",
    )(x, mod, *weights, *tables)


def _half_mask(lane, hh):
    return lane >= HEAD_DIM if hh else lane < HEAD_DIM


def _chunk_update(qh, kc, vc, m, l, acc):
    s = _dot_t(qh, kc)
    m_new = jnp.maximum(m, jnp.max(s, axis=-1, keepdims=True))
    alpha = jnp.exp2(m - m_new)
    pr = jnp.exp2(s - m_new)
    l = alpha * l + jnp.sum(pr, axis=-1, keepdims=True)
    acc = alpha * acc + _dot(pr.astype(BF16), vc)
    return m_new, l, acc


def _attn_kernel(*refs, masked, n_lat, chunk):
    if n_lat:
        q_ref, kl_ref, vl_ref, kc_ref, vc_ref, o_ref = refs
    else:
        q_ref, kc_ref, vc_ref, o_ref = refs
    tq = q_ref.shape[1]
    lane = lax.broadcasted_iota(jnp.int32, (tq, LANES), 1)
    qs, ksls = [], []
    for hh in range(2):
        if masked:
            qs.append(jnp.where(_half_mask(lane, hh), q_ref[0], jnp.zeros((), BF16)))
            ksls.append(slice(None))
        else:
            qs.append(q_ref[0, :, LANES * hh:LANES * (hh + 1)])
            ksls.append(slice(LANES * hh, LANES * (hh + 1)))
    carries = [(jnp.full((tq, 1), NEG_BIG, F32), jnp.zeros((tq, 1), F32), jnp.zeros((tq, LANES), F32))
               for _ in range(2)]
    for c0 in range(0, n_lat, chunk):
        rows = pl.ds(c0, chunk)
        for hh in range(2):
            carries[hh] = _chunk_update(qs[hh], kl_ref[0, rows, ksls[hh]], vl_ref[0, rows, :], *carries[hh])
    outs = []
    for hh in range(2):
        m, l, acc = _chunk_update(qs[hh], kc_ref[0, :, ksls[hh]], vc_ref[0], *carries[hh])
        outs.append(acc * (1.0 / l))
    o_ref[0] = jnp.where(lane < HEAD_DIM, outs[0], outs[1]).astype(BF16)


def _attention(q, k_lat, v_lat, k_ctx, v_ctx, *, mode, tq, chunk=1024):
    b, nq, _ = q.shape
    masked = mode != "B"
    qw = 256 if mode == "B" else LANES
    kw = 256 if mode == "B" else LANES
    kidx = (lambda bb, p, t: (bb, 0, 0)) if mode == "D" else (lambda bb, p, t: (bb, 0, p))
    n_ctx = k_ctx.shape[1]
    n_lat = 0 if k_lat is None else k_lat.shape[1]
    in_specs = [pl.BlockSpec((1, tq, qw), lambda bb, p, t: (bb, t, p))]
    args = [q]
    if n_lat:
        in_specs += [pl.BlockSpec((1, n_lat, kw), kidx), pl.BlockSpec((1, n_lat, LANES), kidx)]
        args += [k_lat, v_lat]
    in_specs += [pl.BlockSpec((1, n_ctx, kw), kidx), pl.BlockSpec((1, n_ctx, LANES), kidx)]
    args += [k_ctx, v_ctx]
    return pl.pallas_call(
        functools.partial(_attn_kernel, masked=masked, n_lat=n_lat, chunk=chunk),
        grid=(b, 2, nq // tq),
        in_specs=in_specs,
        out_specs=pl.BlockSpec((1, tq, LANES), lambda bb, p, t: (bb, t, p)),
        out_shape=jax.ShapeDtypeStruct((b, nq, 256), BF16),
        compiler_params=_params(3),
        name="attn_" + mode + ("_lat" if n_lat else "_ctx"),
    )(*args)


def _with_ones_row(vt, hh):
    row = lax.broadcasted_iota(jnp.int32, vt.shape, 0)
    return jnp.where(row == _ones_row(hh), jnp.ones((), BF16), vt)


def _ones_row(hh):
    return 0 if hh else HEAD_DIM


def _softmax_update_t(s, vtc, m, acc):
    m_new = jnp.maximum(m, jnp.max(s, axis=0, keepdims=True))
    alpha = jnp.exp2(m - m_new)
    pr = jnp.exp2(s - m_new)
    acc = alpha * acc + _dot(vtc, pr.astype(BF16))
    return m_new, acc


def _finish_t(accs):
    outs = [acc * (1.0 / acc[_ones_row(hh):_ones_row(hh) + 1, :]) for hh, acc in enumerate(accs)]
    row = lax.broadcasted_iota(jnp.int32, outs[0].shape, 0)
    return jnp.where(row < HEAD_DIM, outs[0], outs[1]).T.astype(BF16)


ATTN_AHEAD = 4


def _attn_t_kernel(q_ref, kl_ref, vtl_ref, kc_ref, vtc_ref, o_ref, *, masked, chunk):
    tq = q_ref.shape[1]
    n_lat = kl_ref.shape[1]
    lane = lax.broadcasted_iota(jnp.int32, (tq, LANES), 1)
    qs, ksls = [], []
    for hh in range(2):
        if masked:
            qs.append(jnp.where(_half_mask(lane, hh), q_ref[0], jnp.zeros((), BF16)))
            ksls.append(slice(None))
        else:
            qs.append(q_ref[0, :, LANES * hh:LANES * (hh + 1)])
            ksls.append(slice(LANES * hh, LANES * (hh + 1)))
    carries = [(jnp.full((1, tq), NEG_BIG, F32), jnp.zeros((LANES, tq), F32)) for _ in range(2)]
    tasks = [(hh, (kl_ref, pl.ds(c0, chunk)), (vtl_ref, pl.ds(c0, chunk)))
             for c0 in range(0, n_lat, chunk) for hh in range(2)]
    tasks += [(hh, (kc_ref, slice(None)), (vtc_ref, slice(None))) for hh in range(2)]

    def scores(task):
        hh, (k_ref, rows), _ = task
        return _dot_t(k_ref[0, rows, ksls[hh]], qs[hh])

    ahead = [scores(task) for task in tasks[:ATTN_AHEAD]]
    for i, task in enumerate(tasks):
        s_cur = ahead.pop(0)
        if i + ATTN_AHEAD < len(tasks):
            ahead.append(scores(tasks[i + ATTN_AHEAD]))
        hh, _, (vt_ref, cols) = task
        carries[hh] = _softmax_update_t(s_cur, _with_ones_row(vt_ref[0, :, cols], hh), *carries[hh])
    o_ref[0] = _finish_t([acc for _, acc in carries])


def _attention_t(q, k_lat, vt_lat, k_ctx, vt_ctx, *, mode, tq, chunk=512):
    b, nq, _ = q.shape
    masked = mode != "B"
    qw = 256 if mode == "B" else LANES
    kidx = (lambda bb, p, t: (bb, 0, 0)) if mode == "D" else (lambda bb, p, t: (bb, 0, p))
    vidx = (lambda bb, p, t: (bb, 0, 0)) if mode == "D" else (lambda bb, p, t: (bb, p, 0))
    n_lat, n_ctx = k_lat.shape[1], k_ctx.shape[1]
    return pl.pallas_call(
        functools.partial(_attn_t_kernel, masked=masked, chunk=chunk),
        grid=(b, 2, nq // tq),
        in_specs=[pl.BlockSpec((1, tq, qw), lambda bb, p, t: (bb, t, p)),
                  pl.BlockSpec((1, n_lat, qw), kidx), pl.BlockSpec((1, LANES, n_lat), vidx),
                  pl.BlockSpec((1, n_ctx, qw), kidx), pl.BlockSpec((1, LANES, n_ctx), vidx)],
        out_specs=pl.BlockSpec((1, tq, LANES), lambda bb, p, t: (bb, t, p)),
        out_shape=jax.ShapeDtypeStruct((b, nq, 256), BF16),
        compiler_params=_params(3),
        name="attn_" + mode + "_lat",
    )(q, k_lat, vt_lat, k_ctx, vt_ctx)


NA_QROWS = 4
NA_KROWS = 12
NA_TQ = NA_QROWS * GRID_W
NA_KBLK = 256
NA_KPIECES = NA_KROWS * GRID_W // NA_KBLK
NA_SUBS = 2


def _na_kernel(q_ref, *refs):
    k_refs, refs = refs[:NA_SUBS * NA_KPIECES], refs[NA_SUBS * NA_KPIECES:]
    vt_refs, refs = refs[:NA_SUBS * NA_KPIECES], refs[NA_SUBS * NA_KPIECES:]
    kc_ref, vtc_ref = refs[:2]
    bias_refs, o_ref = refs[2:2 + NA_SUBS], refs[2 + NA_SUBS]
    lane = lax.broadcasted_iota(jnp.int32, (NA_TQ, LANES), 1)
    work = []
    for sub in range(NA_SUBS):
        q = q_ref[0, pl.ds(sub * NA_TQ, NA_TQ), :]
        pieces = slice(sub * NA_KPIECES, (sub + 1) * NA_KPIECES)
        k_win = jnp.concatenate([r[0] for r in k_refs[pieces]], axis=0)
        vt_win = jnp.concatenate([r[0] for r in vt_refs[pieces]], axis=1)
        for hh in range(2):
            qh = jnp.where(_half_mask(lane, hh), q, jnp.zeros((), BF16))
            work.append((vt_win, hh, _dot_t(k_win, qh) + bias_refs[sub][hh, 0], _dot_t(kc_ref[0], qh)))
    accs = []
    for vt_win, hh, s_win, s_ctx in work:
        m = jnp.maximum(jnp.max(s_win, axis=0, keepdims=True), jnp.max(s_ctx, axis=0, keepdims=True))
        p_win = jnp.exp2(s_win - m).astype(BF16)
        p_ctx = jnp.exp2(s_ctx - m).astype(BF16)
        accs.append(_dot(_with_ones_row(vt_win, hh), p_win) + _dot(_with_ones_row(vtc_ref[0], hh), p_ctx))
    for sub in range(NA_SUBS):
        o_ref[0, pl.ds(sub * NA_TQ, NA_TQ), :] = _finish_t(accs[2 * sub:2 * sub + 2])


def _na_bias(rel_bias):
    h, n_dr, n_dc = rel_bias.shape
    n_rows = 64
    w2 = 2 * GRID_W
    pad_lo = GRID_W - WIN_W
    p = jnp.pad(rel_bias * LOG2E, ((0, 0), (0, 0), (pad_lo, w2 - pad_lo - n_dc)))
    skew = jnp.broadcast_to(p[:, :, None, :], (h, n_dr, GRID_W, w2)).reshape(h, n_dr, GRID_W * w2)
    skew = skew[:, :, :GRID_W * (w2 - 1)].reshape(h, n_dr, GRID_W, w2 - 1)
    tiles = skew[..., GRID_W - 1:]
    qc = np.arange(GRID_W)
    cs = np.clip(qc - WIN_W // 2, 0, GRID_W - WIN_W)
    in_col = (qc[None, :] >= cs[:, None]) & (qc[None, :] < cs[:, None] + WIN_W)
    tiles = jnp.where(in_col, tiles, NEG_BIG)
    tiles = jnp.concatenate([tiles, jnp.full((h, 1, GRID_W, GRID_W), NEG_BIG, F32)], axis=1)
    ids = np.full((3, NA_KROWS, NA_QROWS), n_dr, np.int32)
    for v, t in enumerate((0, 1, n_rows // NA_QROWS - 1)):
        kb = int(np.clip(NA_QROWS * t - WIN_H // 2, 0, n_rows - NA_KROWS))
        for i in range(NA_QROWS):
            qr = NA_QROWS * t + i
            rs = int(np.clip(qr - WIN_H // 2, 0, n_rows - WIN_H))
            for kr in range(rs, rs + WIN_H):
                ids[v, kr - kb, i] = kr - qr + WIN_H - 1
    tiles_t = jnp.swapaxes(tiles, -1, -2)
    n_v = ids.shape[0]
    return pl.pallas_call(
        _na_bias_kernel,
        grid_spec=pltpu.PrefetchScalarGridSpec(
            num_scalar_prefetch=1,
            grid=(h, n_v),
            in_specs=[pl.BlockSpec((1, n_dr + 1, GRID_W, GRID_W), lambda hh, v, ids_ref: (hh, 0, 0, 0))],
            out_specs=pl.BlockSpec((1, 1, NA_KROWS * GRID_W, NA_TQ), lambda hh, v, ids_ref: (hh, v, 0, 0))),
        out_shape=jax.ShapeDtypeStruct((h, n_v, NA_KROWS * GRID_W, NA_TQ), F32),
        compiler_params=_params(2),
        name="na_bias",
    )(jnp.asarray(ids.reshape(-1)), tiles_t)


def _na_bias_kernel(ids_ref, tiles_ref, o_ref):
    v = pl.program_id(1)
    for kr in range(NA_KROWS):
        for i in range(0, NA_QROWS, 2):
            base = (v * NA_KROWS + kr) * NA_QROWS + i
            pair = jnp.concatenate([tiles_ref[0, ids_ref[base]], tiles_ref[0, ids_ref[base + 1]]], axis=-1)
            o_ref[0, 0, pl.ds(GRID_W * kr, GRID_W), pl.ds(GRID_W * i, 2 * GRID_W)] = pair


def _na_attention(q, k, vt, k_ctx, vt_ctx, bias):
    b, n, _ = q.shape
    step_q = NA_SUBS * NA_TQ
    nj = n // step_q
    n_sub = n // NA_TQ
    n_kblk = n // NA_KBLK
    n_ctx = k_ctx.shape[1]
    rows_per_blk = NA_KBLK // GRID_W

    def kstart(j, sub):
        t = NA_SUBS * j + sub
        return jnp.clip((NA_QROWS * t - WIN_H // 2) // rows_per_blk, 0, n_kblk - NA_KPIECES)

    def kspec(sub, i):
        return pl.BlockSpec((1, NA_KBLK, LANES), lambda p, j, bb: (bb, kstart(j, sub) + i, p))

    def vspec(sub, i):
        return pl.BlockSpec((1, LANES, NA_KBLK), lambda p, j, bb: (bb, p, kstart(j, sub) + i))

    def bspec(sub):
        def variant(j):
            t = NA_SUBS * j + sub
            return jnp.where(t == 0, 0, jnp.where(t == n_sub - 1, 2, 1))
        return pl.BlockSpec((2, 1, NA_KROWS * GRID_W, NA_TQ), lambda p, j, bb: (p, variant(j), 0, 0))

    subs_pieces = [(sub, i) for sub in range(NA_SUBS) for i in range(NA_KPIECES)]
    return pl.pallas_call(
        _na_kernel,
        grid=(2, nj, b),
        in_specs=[pl.BlockSpec((1, step_q, LANES), lambda p, j, bb: (bb, j, p))]
        + [kspec(sub, i) for sub, i in subs_pieces] + [vspec(sub, i) for sub, i in subs_pieces]
        + [pl.BlockSpec((1, n_ctx, LANES), lambda p, j, bb: (bb, 0, p)),
           pl.BlockSpec((1, LANES, n_ctx), lambda p, j, bb: (bb, p, 0))]
        + [bspec(sub) for sub in range(NA_SUBS)],
        out_specs=pl.BlockSpec((1, step_q, LANES), lambda p, j, bb: (bb, j, p)),
        out_shape=jax.ShapeDtypeStruct((b, n, 256), BF16),
        compiler_params=_params(3),
        name="na_lat",
    )(q, *([k] * len(subs_pieces)), *([vt] * len(subs_pieces)), k_ctx, vt_ctx, *([bias] * NA_SUBS))


LRU_SEG = 8
LRU_PAD = 8
LRU_ROWS = 512
LRU_HALVES = 2


def _pitch(n):
    seg = n // LRU_SEG
    return seg + 8 if (seg // 8) % 2 == 0 else seg


def _pieces(n, t0, r):
    seg, out, t = n // LRU_SEG, [], t0
    while t < t0 + r:
        length = min(seg - t % seg, t0 + r - t)
        out.append((t - t0, (t // seg) * _pitch(n) + t % seg, length))
        t += length
    return out


def _put(ref, n, t0, val):
    for off, row, length in _pieces(n, t0, val.shape[0]):
        for hv in range(LRU_HALVES):
            ref[hv, pl.ds(row, length), :] = val[off:off + length, LANES * hv:LANES * (hv + 1)]


def _get(ref, n, t0, r):
    return jnp.concatenate(
        [jnp.concatenate([ref[hv, pl.ds(row, length), :] for hv in range(LRU_HALVES)], axis=-1)
         for _, row, length in _pieces(n, t0, r)], axis=0)


def _lru_gates(xp, n, cw, cb, wa_ref, ba_ref, wx_ref, bx_ref, sp, a_refs, u_refs):
    r = min(LRU_ROWS, n)
    for c0 in range(0, n, r):
        conv = cb
        for tap in range(CONV_W):
            conv = conv + cw[tap:tap + 1, :] * xp[pl.ds(LRU_PAD + c0 + tap - CONV_W // 2, r), :]
        xb = conv.astype(BF16)
        for d in range(2):
            rg = _sigmoid(_dot(xb, wa_ref[d]) + ba_ref[d:d + 1, :])
            ig = _sigmoid(_dot(xb, wx_ref[d]) + bx_ref[d:d + 1, :])
            a = jnp.exp((-LRU_C) * rg * sp[d:d + 1, :])
            _put(a_refs[d], n, c0, a)
            _put(u_refs[d], n, c0, jnp.sqrt(1.0 - a * a) * (ig * conv))


def _lru_scan(n, af, uf, ab, ub, hf0, hb0):
    seg, pitch = n // LRU_SEG, _pitch(n)

    def body(i, carry):
        rf = pl.ds(i, LRU_SEG, stride=pitch)
        rb = pl.ds(seg - 1 - i, LRU_SEG, stride=pitch)
        out = []
        for hv in range(LRU_HALVES):
            hlf, cmf, hlb, cmb = carry[4 * hv:4 * hv + 4]
            a1, u1 = af[hv, rf, :], uf[hv, rf, :]
            a2, u2 = ab[hv, rb, :], ub[hv, rb, :]
            hlf = a1 * hlf + u1
            cmf = a1 * cmf
            hlb = a2 * hlb + u2
            cmb = a2 * cmb
            uf[hv, rf, :] = hlf
            af[hv, rf, :] = cmf
            ub[hv, rb, :] = hlb
            ab[hv, rb, :] = cmb
            out += [hlf, cmf, hlb, cmb]
        return tuple(out)

    zero = jnp.zeros((LRU_SEG, LANES), F32)
    one = jnp.ones((LRU_SEG, LANES), F32)
    lax.fori_loop(0, seg, body, (zero, one, zero, one) * LRU_HALVES)
    hf, hb = list(hf0), list(hb0)
    for s in range(LRU_SEG):
        rf = pl.ds(s * pitch, seg)
        rb = pl.ds((LRU_SEG - 1 - s) * pitch, seg)
        for hv in range(LRU_HALVES):
            h = uf[hv, rf, :] + af[hv, rf, :] * hf[hv]
            uf[hv, rf, :] = h
            hf[hv] = h[seg - 1:seg, :]
            h = ub[hv, rb, :] + ab[hv, rb, :] * hb[hv]
            ub[hv, rb, :] = h
            hb[hv] = h[0:1, :]
    return hf, hb


def _lru_kernel(xl_ref, xc_ref, cw_ref, cb_ref, wa_ref, ba_ref, wx_ref, bx_ref, lam_ref,
                yl_ref, yc_ref, xp, afl, ufl, abl, ubl, afc, ufc, abc, ubc):
    n_lat, w = xl_ref.shape[1], xl_ref.shape[2]
    n_ctx = xc_ref.shape[1]
    cw, cb = cw_ref[...], cb_ref[...]
    neg_lam = -lam_ref[...]
    sp = jnp.maximum(neg_lam, 0.0) + jnp.log1p(jnp.exp(-jnp.abs(neg_lam)))
    zpad = jnp.zeros((LRU_PAD, w), F32)

    xp[pl.ds(0, LRU_PAD), :] = zpad
    xp[pl.ds(LRU_PAD, n_ctx), :] = xc_ref[0]
    xp[pl.ds(LRU_PAD + n_ctx, LRU_PAD), :] = zpad
    _lru_gates(xp, n_ctx, cw, cb, wa_ref, ba_ref, wx_ref, bx_ref, sp, (afc, abc), (ufc, ubc))
    xp[pl.ds(LRU_PAD, n_lat), :] = xl_ref[0]
    xp[pl.ds(LRU_PAD + n_lat, LRU_PAD), :] = zpad
    _lru_gates(xp, n_lat, cw, cb, wa_ref, ba_ref, wx_ref, bx_ref, sp, (afl, abl), (ufl, ubl))

    h0 = [jnp.zeros((1, LANES), F32)] * LRU_HALVES
    hf, hb = _lru_scan(n_ctx, afc, ufc, abc, ubc, h0, h0)
    _lru_scan(n_lat, afl, ufl, abl, ubl, hf, hb)
    yc_ref[0] = (_get(ufc, n_ctx, 0, n_ctx) + _get(ubc, n_ctx, 0, n_ctx)).astype(BF16)
    for c0 in range(0, n_lat, LRU_ROWS):
        yl_ref[0, pl.ds(c0, LRU_ROWS), :] = (_get(ufl, n_lat, c0, LRU_ROWS)
                                             + _get(ubl, n_lat, c0, LRU_ROWS)).astype(BF16)


def _lru(x_lat, x_ctx, conv_w, conv_b, wa, ba, wx, bx, lam):
    b, n_lat, w = x_lat.shape
    n_ctx = x_ctx.shape[1]
    consts = (conv_w, conv_b, wa, ba, wx, bx, lam)
    big = lambda: pltpu.VMEM((LRU_HALVES, LRU_SEG * _pitch(n_lat), LANES), F32)
    small = lambda: pltpu.VMEM((LRU_HALVES, LRU_SEG * _pitch(n_ctx), LANES), F32)
    return pl.pallas_call(
        _lru_kernel,
        grid=(b,),
        in_specs=[pl.BlockSpec((1, n_lat, w), lambda bb: (bb, 0, 0)),
                  pl.BlockSpec((1, n_ctx, w), lambda bb: (bb, 0, 0))]
        + [_const_spec(c.shape) for c in consts],
        out_specs=[pl.BlockSpec((1, n_lat, w), lambda bb: (bb, 0, 0)),
                   pl.BlockSpec((1, n_ctx, w), lambda bb: (bb, 0, 0))],
        out_shape=[jax.ShapeDtypeStruct((b, n_lat, w), BF16),
                   jax.ShapeDtypeStruct((b, n_ctx, w), BF16)],
        scratch_shapes=[pltpu.VMEM((n_lat + 2 * LRU_PAD, w), F32),
                        big(), big(), big(), big(), small(), small(), small(), small()],
        compiler_params=_params(1),
        name="rglru",
    )(x_lat, x_ctx, *consts)


MERGE_SUB_ROWS = 256


def _merge_kernel(x_ref, mod_ref, ya_ref, yb_ref, yc_ref, yd_ref, wz_ref, wm_ref, wb_ref, wo_ref,
                  lng_ref, lnb_ref, o_ref):
    shift = mod_ref[0, 0:1, :]
    scale = mod_ref[0, 1:2, :]
    gate = mod_ref[0, 2:3, :]
    tile = x_ref.shape[1]
    sub = min(tile, MERGE_SUB_ROWS)
    for r0 in range(0, tile, sub):
        rows = pl.ds(r0, sub)
        x = x_ref[0, rows, :]
        xm = (_layer_norm(x) * (1.0 + scale) + shift).astype(BF16)
        z = _dot(xm, wz_ref[...])
        acc = None
        for i, y_ref in enumerate((ya_ref, yb_ref, yc_ref, yd_ref)):
            zi = z[:, BRANCH_W * i:BRANCH_W * (i + 1)]
            g = (y_ref[0, rows, :].astype(F32) * (zi * _sigmoid(zi))).astype(BF16)
            t = _dot(g, wb_ref[i])
            mi = _dot(xm, wm_ref[:, D_MODEL * i:D_MODEL * (i + 1)])
            term = _sigmoid(mi) * t
            acc = term if acc is None else acc + term
        out = _dot(acc.astype(BF16), wo_ref[...])
        o_ref[0, rows, :] = _layer_norm(DEEPNORM_ALPHA * x + gate * out) * lng_ref[...] + lnb_ref[...]


def _merge(x, mod, ys, wz, wm, wb, wo, ln_g, ln_b, tile):
    b, n, _ = x.shape
    consts = (wz, wm, wb, wo, ln_g, ln_b)
    tok = lambda w: pl.BlockSpec((1, tile, w), lambda bb, t: (bb, t, 0))
    return pl.pallas_call(
        _merge_kernel,
        grid=(b, n // tile),
        in_specs=[tok(D_MODEL), pl.BlockSpec((1, 3, D_MODEL), lambda bb, t: (bb, 0, 0))]
        + [tok(BRANCH_W) for _ in ys] + [_const_spec(c.shape, single_buffer=True) for c in consts],
        out_specs=tok(D_MODEL),
        out_shape=jax.ShapeDtypeStruct((b, n, D_MODEL), F32),
        compiler_params=_params(2),
        name="merge",
    )(x, mod, *ys, *consts)


def _rope_tables(n):
    t = jnp.arange(n, dtype=jnp.int32)
    rows = (t // GRID_W).astype(F32)[:, None]
    cols = (t % GRID_W).astype(F32)[:, None]

    def table(half, offsets):
        inv = ROPE_THETA ** (-np.arange(0, 2 * half, 2, dtype=np.float64) / (2 * half))
        f_row, f_col, sign = (np.zeros(LANES, np.float32) for _ in range(3))
        for offset in offsets:
            for g, f in enumerate((f_row, f_col)):
                base = offset + 2 * half * g
                f[base:base + 2 * half] = np.concatenate([inv, inv])
                sign[base:base + 2 * half] = np.concatenate([-np.ones(half), np.ones(half)])
        ang = rows * f_row[None, :] + cols * f_col[None, :]
        return jnp.cos(ang), jnp.sin(ang) * sign[None, :]

    cosd, sind = table(16, (0, HEAD_DIM))
    cosb, sinb = table(8, (MLA_NOPE,))
    cosk, sink = table(8, (0,))
    return cosd, sind, cosb, sinb, cosk, sink


def _identity_tables(n):
    one, zero = jnp.ones((n, LANES), F32), jnp.zeros((n, LANES), F32)
    return one, zero, one, zero, one, zero


def _layer_weights(w_in, mla_q_norm, mla_w_uq, mla_kv_norm, mla_w_ukv, gqa_q_norm, gqa_k_norm, w_branch, w_out):
    w_in = w_in.astype(BF16)
    wi = w_in[:, :MIX_COLS]
    qa, ka, va = wi[:, 0:256], wi[:, 256:512], wi[:, 512:768]
    cq, ckv, kr = wi[:, 768:1024], wi[:, 1024:1152], wi[:, 1152:1184]
    lru, qd, kd, vd = wi[:, 1184:1440], wi[:, 1440:1696], wi[:, 1696:1824], wi[:, 1824:1952]
    head_order = jnp.array([0, 2, 1, 3])
    qd = qd.reshape(D_MODEL, 4, HEAD_DIM)[:, head_order].reshape(D_MODEL, 256)
    wmix = jnp.concatenate([qa, ka, va, cq, ckv, lru, qd, kd, vd, kr,
                            jnp.zeros((D_MODEL, MIX_PAD - MIX_COLS), BF16)], -1)
    hq = MLA_NOPE + MLA_ROPE
    wuq = jnp.pad(mla_w_uq.reshape(256, MLA_HEADS, hq), ((0, 0), (0, 0), (0, LANES - hq)))
    wuq = wuq.reshape(256, MLA_HEADS * LANES).astype(BF16)
    ukv = mla_w_ukv.reshape(128, MLA_HEADS, 128)
    wuk = jnp.pad(ukv[:, :, :MLA_NOPE], ((0, 0), (0, 0), (0, LANES - MLA_NOPE)))
    wuk = wuk.reshape(128, MLA_HEADS * LANES).astype(BF16)
    wuv = ukv[:, :, MLA_NOPE:].reshape(128, 256).astype(BF16)
    pk = np.zeros((LANES, MLA_HEADS * LANES), np.float32)
    for h in range(MLA_HEADS):
        pk[np.arange(MLA_ROPE), LANES * h + MLA_NOPE + np.arange(MLA_ROPE)] = 1.0
    grp = np.kron(np.eye(4, dtype=np.float32), np.ones((HEAD_DIM, HEAD_DIM), np.float32))
    inproj_w = (wmix, wuq, wuk, wuv, jnp.asarray(pk, BF16), jnp.asarray(grp, BF16),
                mla_q_norm.reshape(1, 256), mla_kv_norm.reshape(1, 128),
                jnp.tile(gqa_q_norm, 4).reshape(1, 256), jnp.tile(gqa_k_norm, 2).reshape(1, 128))
    wz = w_in[:, MIX_COLS:MIX_COLS + SILU_COLS]
    wz_d = wz[:, 3 * BRANCH_W:].reshape(D_MODEL, 4, HEAD_DIM)[:, head_order].reshape(D_MODEL, BRANCH_W)
    wz = jnp.concatenate([wz[:, :3 * BRANCH_W], wz_d], -1)
    wm = w_in[:, MIX_COLS + SILU_COLS:]
    wb_d = w_branch[3].reshape(4, HEAD_DIM, D_MODEL)[head_order].reshape(BRANCH_W, D_MODEL)
    wb = jnp.concatenate([w_branch[:3], wb_d[None]], 0).astype(BF16)
    return inproj_w, (wz, wm, wb, w_out.astype(BF16))


def _block_diag(w):
    eye = jnp.eye(4, dtype=w.dtype)
    return jnp.einsum("dkce,kj->dkcje", w, eye).reshape(2, 256, 256)


def kernel(x, c, ctx, c_ctx, w_mod, b_mod, w_in, na_rel_bias, mla_q_norm, mla_w_uq, mla_kv_norm, mla_w_ukv,
           lru_conv_w, lru_conv_b, lru_w_a, lru_b_a, lru_w_x, lru_b_x, lru_lambda, gqa_q_norm, gqa_k_norm,
           w_branch, w_out, ln_g, ln_b):
    b, n, _ = x.shape
    n_ctx = ctx.shape[1]
    rows = 8 * ((b + 1 + 7) // 8)
    c_all = jnp.zeros((rows, D_MODEL), F32).at[:b].set(c).at[b].set(c_ctx)
    mod_all = _modulation(c_all, w_mod, b_mod)
    lat_tables = _rope_tables(n)
    ctx_tables = _identity_tables(n_ctx)

    for l in range(DEPTH):
        need_ctx = l < DEPTH - 1
        mod_lat = mod_all[l, :b].reshape(b, 3, D_MODEL)
        mod_ctx = jnp.broadcast_to(mod_all[l, b].reshape(1, 3, D_MODEL), (b, 3, D_MODEL))
        inproj_w, merge_w = _layer_weights(w_in[l], mla_q_norm[l], mla_w_uq[l], mla_kv_norm[l], mla_w_ukv[l],
                                           gqa_q_norm[l], gqa_k_norm[l], w_branch[l], w_out[l])
        qa, ka, va, qb, kb, vb, xr, qd, kd, vd, vat, vbt, vdt = _inproj(x, mod_lat, inproj_w, lat_tables, 1024)
        qa_c, ka_c, va_c, qb_c, kb_c, vb_c, xr_c, qd_c, kd_c, vd_c, vat_c, vbt_c, vdt_c = _inproj(
            ctx, mod_ctx, inproj_w, ctx_tables, n_ctx)

        ya = _na_attention(qa, ka, vat, ka_c, vat_c, _na_bias(na_rel_bias[l]))
        yb = _attention_t(qb, kb, vbt, kb_c, vbt_c, mode="B", tq=1024)
        yd = _attention_t(qd, kd, vdt, kd_c, vdt_c, mode="D", tq=1024)
        yc, yc_c = _lru(xr, xr_c, lru_conv_w[l], lru_conv_b[l].reshape(1, -1),
                        _block_diag(lru_w_a[l]).astype(BF16), lru_b_a[l],
                        _block_diag(lru_w_x[l]).astype(BF16), lru_b_x[l], lru_lambda[l])
        tail = (*merge_w, ln_g[l].reshape(1, -1), ln_b[l].reshape(1, -1))
        x_new = _merge(x, mod_lat, (ya, yb, yc, yd), *tail, 1024)
        if need_ctx:
            ya_c = _attention(qa_c, None, None, ka_c, va_c, mode="A", tq=n_ctx)
            yb_c = _attention(qb_c, None, None, kb_c, vb_c, mode="B", tq=n_ctx)
            yd_c = _attention(qd_c, None, None, kd_c, vd_c, mode="D", tq=n_ctx)
            ctx = _merge(ctx, mod_ctx, (ya_c, yb_c, yc_c, yd_c), *tail, n_ctx)
        x = x_new
    return x
```

```python
import functools
import math

import numpy as np
import jax
import jax.numpy as jnp
from jax import lax
from jax.experimental import pallas as pl
from jax.experimental.pallas import tpu as pltpu

F32 = jnp.float32
BF16 = jnp.bfloat16

D_MODEL = 1024
DEPTH = 2
GRID_W = 64
BRANCH_W = 256
HEAD_DIM = 64
WIN_H = 8
WIN_W = 16
MLA_HEADS = 4
MLA_NOPE = 64
MLA_ROPE = 32
LRU_C = 8.0
CONV_W = 4
ROPE_THETA = 10000.0
EPS = 1e-6
LOG2E = 1.4426950408889634
NA_SCALE = HEAD_DIM ** -0.5
MLA_SCALE = (MLA_NOPE + MLA_ROPE) ** -0.5
GQA_SCALE = HEAD_DIM ** -0.5
DEEPNORM_ALPHA = (2 * DEPTH) ** 0.25

MIX_COLS = 1952
SILU_COLS = 1024
LANES = 128
NEG_BIG = -1e30
VMEM_LIMIT = 56 * 1024 * 1024

OFF_QA, OFF_KA, OFF_VA, OFF_CQ, OFF_CKV, OFF_LRU, OFF_QD, OFF_KD, OFF_VD, OFF_KR = (
    0, 256, 512, 768, 1024, 1152, 1408, 1664, 1792, 1920)
MIX_PAD = 2048


def _dot(a, b):
    return jnp.dot(a, b, preferred_element_type=F32)


def _dot_t(a, b):
    return lax.dot_general(a, b, (((1,), (1,)), ((), ())), preferred_element_type=F32)


def _layer_norm(x):
    mu = jnp.mean(x, axis=-1, keepdims=True)
    xc = x - mu
    var = jnp.mean(xc * xc, axis=-1, keepdims=True)
    return xc * lax.rsqrt(var + EPS)


def _sigmoid(x):
    return 0.5 * jnp.tanh(0.5 * x) + 0.5


def _rms(x, g):
    return x * lax.rsqrt(jnp.mean(x * x, axis=-1, keepdims=True) + EPS) * g


def _params(n_grid):
    return pltpu.CompilerParams(dimension_semantics=("arbitrary",) * n_grid,
                                vmem_limit_bytes=VMEM_LIMIT)


def _const_spec(shape, single_buffer=False):
    zeros = (0,) * len(shape)
    mode = pl.Buffered(1) if single_buffer else None
    return pl.BlockSpec(shape, lambda *_: zeros, pipeline_mode=mode)


def _mod_kernel(c_ref, w_ref, b_ref, o_ref):
    c = c_ref[...]
    s = (c * jax.nn.sigmoid(c)).astype(BF16)
    o_ref[0] = _dot(s, w_ref[0].astype(BF16)) + b_ref[0]


def _modulation(c_all, w_mod, b_mod):
    n_l, _, n_out = w_mod.shape
    rows = c_all.shape[0]
    bn = 512
    return pl.pallas_call(
        _mod_kernel,
        grid=(n_l, n_out // bn),
        in_specs=[pl.BlockSpec((rows, D_MODEL), lambda l, n: (0, 0)),
                  pl.BlockSpec((1, D_MODEL, bn), lambda l, n: (l, 0, n)),
                  pl.BlockSpec((1, 1, bn), lambda l, n: (l, 0, n))],
        out_specs=pl.BlockSpec((1, rows, bn), lambda l, n: (l, 0, n)),
        out_shape=jax.ShapeDtypeStruct((n_l, rows, n_out), F32),
        compiler_params=_params(2),
        name="modulation",
    )(c_all, w_mod, b_mod.reshape(n_l, 1, n_out))


def _pair_select(lane, shift):
    lane_up = pltpu.roll(lane, LANES - shift, 1)
    want = jnp.where((lane & (2 * shift - 1)) < shift, lane + shift, lane - shift)
    return lane_up == want


def _rope(x, cos, sin, shift, sel):
    partner = jnp.where(sel, pltpu.roll(x, LANES - shift, 1), pltpu.roll(x, shift, 1))
    return x * cos + partner * sin


def _group_sumsq(x, g):
    sq = x * x
    hi = sq.astype(BF16)
    lo = (sq - hi.astype(F32)).astype(BF16)
    return _dot(hi, g) + _dot(lo, g)


INPROJ_SUB_ROWS = 512


def _inproj_kernel(x_ref, mod_ref, wmix_ref, wuq_ref, wuk_ref, wuv_ref, pk_ref, g_ref,
                   gq_ref, gkv_ref, gdq_ref, gdk_ref,
                   cosd_ref, sind_ref, cosb_ref, sinb_ref, cosk_ref, sink_ref,
                   qa_o, ka_o, va_o, qb_o, kb_o, vb_o, xc_o, qd_o, kd_o, vd_o, vat_o, vbt_o, vdt_o):
    shift = mod_ref[0, 0:1, :]
    scale = mod_ref[0, 1:2, :]
    tile = x_ref.shape[1]
    sub = min(tile, INPROJ_SUB_ROWS)
    lane = lax.broadcasted_iota(jnp.int32, (sub, LANES), 1)
    sel16 = _pair_select(lane, 16)
    sel8 = _pair_select(lane, 8)
    for r0 in range(0, tile, sub):
        rows = pl.ds(r0, sub)
        xm = (_layer_norm(x_ref[0, rows, :]) * (1.0 + scale) + shift).astype(BF16)
        p = _dot(xm, wmix_ref[...])

        qa_o[0, rows, :] = (p[:, OFF_QA:OFF_QA + 256] * (NA_SCALE * LOG2E)).astype(BF16)
        ka_o[0, rows, :] = p[:, OFF_KA:OFF_KA + 256].astype(BF16)
        va = p[:, OFF_VA:OFF_VA + 256]
        va_o[0, rows, :] = va.astype(BF16)
        vat_o[0, :, rows] = va.T.astype(BF16)

        cqn = _rms(p[:, OFF_CQ:OFF_CQ + 256], gq_ref[...]).astype(BF16)
        qb = _dot(cqn, wuq_ref[...])
        cosb, sinb = cosb_ref[rows, :], sinb_ref[rows, :]
        for h in range(MLA_HEADS):
            blk = _rope(qb[:, LANES * h:LANES * (h + 1)], cosb, sinb, 8, sel8)
            qb_o[0, rows, LANES * h:LANES * (h + 1)] = (blk * (MLA_SCALE * LOG2E)).astype(BF16)
        ckvn = _rms(p[:, OFF_CKV:OFF_CKV + 128], gkv_ref[...]).astype(BF16)
        kr = _rope(p[:, OFF_KR:OFF_KR + 128], cosk_ref[rows, :], sink_ref[rows, :], 8, sel8).astype(BF16)
        kb_o[0, rows, :] = (_dot(ckvn, wuk_ref[...]) + _dot(kr, pk_ref[...])).astype(BF16)
        vb = _dot(ckvn, wuv_ref[...])
        vb_o[0, rows, :] = vb.astype(BF16)
        vbt_o[0, :, rows] = vb.T.astype(BF16)

        xc_o[0, rows, :] = p[:, OFF_LRU:OFF_LRU + 256]

        cosd, sind = cosd_ref[rows, :], sind_ref[rows, :]
        qd = p[:, OFF_QD:OFF_QD + 256]
        qd = qd * lax.rsqrt(_group_sumsq(qd, g_ref[...]) * (1.0 / HEAD_DIM) + EPS) * gdq_ref[...]
        for blk_i in range(2):
            blk = _rope(qd[:, LANES * blk_i:LANES * (blk_i + 1)], cosd, sind, 16, sel16)
            qd_o[0, rows, LANES * blk_i:LANES * (blk_i + 1)] = (blk * (GQA_SCALE * LOG2E)).astype(BF16)
        kd = p[:, OFF_KD:OFF_KD + 128]
        kd = kd * lax.rsqrt(_group_sumsq(kd, g_ref[0:128, 0:128]) * (1.0 / HEAD_DIM) + EPS) * gdk_ref[...]
        kd_o[0, rows, :] = _rope(kd, cosd, sind, 16, sel16).astype(BF16)
        vd = p[:, OFF_VD:OFF_VD + 128]
        vd_o[0, rows, :] = vd.astype(BF16)
        vdt_o[0, :, rows] = vd.T.astype(BF16)


def _inproj(x, mod, weights, tables, tile):
    b, n, _ = x.shape
    nt = n // tile
    wspecs = [_const_spec(w.shape) for w in weights]
    tspecs = [pl.BlockSpec((tile, LANES), lambda t, bb: (t, 0)) for _ in tables]
    widths = (256, 256, 256, 512, 512, 256, 256, 256, 128, 128)
    dtypes = (BF16, BF16, BF16, BF16, BF16, BF16, F32, BF16, BF16, BF16)
    t_rows = (256, 256, 128)
    return pl.pallas_call(
        _inproj_kernel,
        grid=(nt, b),
        in_specs=[pl.BlockSpec((1, tile, D_MODEL), lambda t, bb: (bb, t, 0)),
                  pl.BlockSpec((1, 3, D_MODEL), lambda t, bb: (bb, 0, 0))] + wspecs + tspecs,
        out_specs=[pl.BlockSpec((1, tile, w), lambda t, bb: (bb, t, 0)) for w in widths]
        + [pl.BlockSpec((1, r, tile), lambda t, bb: (bb, 0, t)) for r in t_rows],
        out_shape=[jax.ShapeDtypeStruct((b, n, w), dt) for w, dt in zip(widths, dtypes)]
        + [jax.ShapeDtypeStruct((b, r, n), BF16) for r in t_rows],
        compiler_params=_params(2),
        name="inproj",
    )(x, mod, *weights, *tables)


def _half_mask(lane, hh):
    return lane >= HEAD_DIM if hh else lane < HEAD_DIM


def _chunk_update(qh, kc, vc, m, l, acc):
    s = _dot_t(qh, kc)
    m_new = jnp.maximum(m, jnp.max(s, axis=-1, keepdims=True))
    alpha = jnp.exp2(m - m_new)
    pr = jnp.exp2(s - m_new)
    l = alpha * l + jnp.sum(pr, axis=-1, keepdims=True)
    acc = alpha * acc + _dot(pr.astype(BF16), vc)
    return m_new, l, acc


def _attn_kernel(*refs, masked, n_lat, chunk):
    if n_lat:
        q_ref, kl_ref, vl_ref, kc_ref, vc_ref, o_ref = refs
    else:
        q_ref, kc_ref, vc_ref, o_ref = refs
    tq = q_ref.shape[1]
    lane = lax.broadcasted_iota(jnp.int32, (tq, LANES), 1)
    qs, ksls = [], []
    for hh in range(2):
        if masked:
            qs.append(jnp.where(_half_mask(lane, hh), q_ref[0], jnp.zeros((), BF16)))
            ksls.append(slice(None))
        else:
            qs.append(q_ref[0, :, LANES * hh:LANES * (hh + 1)])
            ksls.append(slice(LANES * hh, LANES * (hh + 1)))
    carries = [(jnp.full((tq, 1), NEG_BIG, F32), jnp.zeros((tq, 1), F32), jnp.zeros((tq, LANES), F32))
               for _ in range(2)]
    for c0 in range(0, n_lat, chunk):
        rows = pl.ds(c0, chunk)
        for hh in range(2):
            carries[hh] = _chunk_update(qs[hh], kl_ref[0, rows, ksls[hh]], vl_ref[0, rows, :], *carries[hh])
    outs = []
    for hh in range(2):
        m, l, acc = _chunk_update(qs[hh], kc_ref[0, :, ksls[hh]], vc_ref[0], *carries[hh])
        outs.append(acc * (1.0 / l))
    o_ref[0] = jnp.where(lane < HEAD_DIM, outs[0], outs[1]).astype(BF16)


def _attention(q, k_lat, v_lat, k_ctx, v_ctx, *, mode, tq, chunk=1024):
    b, nq, _ = q.shape
    masked = mode != "B"
    qw = 256 if mode == "B" else LANES
    kw = 256 if mode == "B" else LANES
    kidx = (lambda bb, p, t: (bb, 0, 0)) if mode == "D" else (lambda bb, p, t: (bb, 0, p))
    n_ctx = k_ctx.shape[1]
    n_lat = 0 if k_lat is None else k_lat.shape[1]
    in_specs = [pl.BlockSpec((1, tq, qw), lambda bb, p, t: (bb, t, p))]
    args = [q]
    if n_lat:
        in_specs += [pl.BlockSpec((1, n_lat, kw), kidx), pl.BlockSpec((1, n_lat, LANES), kidx)]
        args += [k_lat, v_lat]
    in_specs += [pl.BlockSpec((1, n_ctx, kw), kidx), pl.BlockSpec((1, n_ctx, LANES), kidx)]
    args += [k_ctx, v_ctx]
    return pl.pallas_call(
        functools.partial(_attn_kernel, masked=masked, n_lat=n_lat, chunk=chunk),
        grid=(b, 2, nq // tq),
        in_specs=in_specs,
        out_specs=pl.BlockSpec((1, tq, LANES), lambda bb, p, t: (bb, t, p)),
        out_shape=jax.ShapeDtypeStruct((b, nq, 256), BF16),
        compiler_params=_params(3),
        name="attn_" + mode + ("_lat" if n_lat else "_ctx"),
    )(*args)


def _with_ones_row(vt, hh):
    row = lax.broadcasted_iota(jnp.int32, vt.shape, 0)
    return jnp.where(row == _ones_row(hh), jnp.ones((), BF16), vt)


def _ones_row(hh):
    return 0 if hh else HEAD_DIM


def _softmax_update_t(s, vtc, m, acc):
    m_new = jnp.maximum(m, jnp.max(s, axis=0, keepdims=True))
    alpha = jnp.exp2(m - m_new)
    pr = jnp.exp2(s - m_new)
    acc = alpha * acc + _dot(vtc, pr.astype(BF16))
    return m_new, acc


def _finish_t(accs):
    outs = [acc * (1.0 / acc[_ones_row(hh):_ones_row(hh) + 1, :]) for hh, acc in enumerate(accs)]
    row = lax.broadcasted_iota(jnp.int32, outs[0].shape, 0)
    return jnp.where(row < HEAD_DIM, outs[0], outs[1]).T.astype(BF16)


ATTN_AHEAD = 4


def _attn_t_kernel(q_ref, kl_ref, vtl_ref, kc_ref, vtc_ref, o_ref, *, masked, chunk):
    tq = q_ref.shape[1]
    n_lat = kl_ref.shape[1]
    lane = lax.broadcasted_iota(jnp.int32, (tq, LANES), 1)
    qs, ksls = [], []
    for hh in range(2):
        if masked:
            qs.append(jnp.where(_half_mask(lane, hh), q_ref[0], jnp.zeros((), BF16)))
            ksls.append(slice(None))
        else:
            qs.append(q_ref[0, :, LANES * hh:LANES * (hh + 1)])
            ksls.append(slice(LANES * hh, LANES * (hh + 1)))
    carries = [(jnp.full((1, tq), NEG_BIG, F32), jnp.zeros((LANES, tq), F32)) for _ in range(2)]
    tasks = [(hh, (kl_ref, pl.ds(c0, chunk)), (vtl_ref, pl.ds(c0, chunk)))
             for c0 in range(0, n_lat, chunk) for hh in range(2)]
    tasks += [(hh, (kc_ref, slice(None)), (vtc_ref, slice(None))) for hh in range(2)]

    def scores(task):
        hh, (k_ref, rows), _ = task
        return _dot_t(k_ref[0, rows, ksls[hh]], qs[hh])

    ahead = [scores(task) for task in tasks[:ATTN_AHEAD]]
    for i, task in enumerate(tasks):
        s_cur = ahead.pop(0)
        if i + ATTN_AHEAD < len(tasks):
            ahead.append(scores(tasks[i + ATTN_AHEAD]))
        hh, _, (vt_ref, cols) = task
        carries[hh] = _softmax_update_t(s_cur, _with_ones_row(vt_ref[0, :, cols], hh), *carries[hh])
    o_ref[0] = _finish_t([acc for _, acc in carries])


def _attention_t(q, k_lat, vt_lat, k_ctx, vt_ctx, *, mode, tq, chunk=512):
    b, nq, _ = q.shape
    masked = mode != "B"
    qw = 256 if mode == "B" else LANES
    kidx = (lambda bb, p, t: (bb, 0, 0)) if mode == "D" else (lambda bb, p, t: (bb, 0, p))
    vidx = (lambda bb, p, t: (bb, 0, 0)) if mode == "D" else (lambda bb, p, t: (bb, p, 0))
    n_lat, n_ctx = k_lat.shape[1], k_ctx.shape[1]
    return pl.pallas_call(
        functools.partial(_attn_t_kernel, masked=masked, chunk=chunk),
        grid=(b, 2, nq // tq),
        in_specs=[pl.BlockSpec((1, tq, qw), lambda bb, p, t: (bb, t, p)),
                  pl.BlockSpec((1, n_lat, qw), kidx), pl.BlockSpec((1, LANES, n_lat), vidx),
                  pl.BlockSpec((1, n_ctx, qw), kidx), pl.BlockSpec((1, LANES, n_ctx), vidx)],
        out_specs=pl.BlockSpec((1, tq, LANES), lambda bb, p, t: (bb, t, p)),
        out_shape=jax.ShapeDtypeStruct((b, nq, 256), BF16),
        compiler_params=_params(3),
        name="attn_" + mode + "_lat",
    )(q, k_lat, vt_lat, k_ctx, vt_ctx)


NA_QROWS = 4
NA_KROWS = 12
NA_TQ = NA_QROWS * GRID_W
NA_KBLK = 256
NA_KPIECES = NA_KROWS * GRID_W // NA_KBLK
NA_SUBS = 2


def _na_kernel(q_ref, *refs):
    k_refs, refs = refs[:NA_SUBS * NA_KPIECES], refs[NA_SUBS * NA_KPIECES:]
    vt_refs, refs = refs[:NA_SUBS * NA_KPIECES], refs[NA_SUBS * NA_KPIECES:]
    kc_ref, vtc_ref = refs[:2]
    bias_refs, o_ref = refs[2:2 + NA_SUBS], refs[2 + NA_SUBS]
    lane = lax.broadcasted_iota(jnp.int32, (NA_TQ, LANES), 1)
    work = []
    for sub in range(NA_SUBS):
        q = q_ref[0, pl.ds(sub * NA_TQ, NA_TQ), :]
        pieces = slice(sub * NA_KPIECES, (sub + 1) * NA_KPIECES)
        k_win = jnp.concatenate([r[0] for r in k_refs[pieces]], axis=0)
        vt_win = jnp.concatenate([r[0] for r in vt_refs[pieces]], axis=1)
        for hh in range(2):
            qh = jnp.where(_half_mask(lane, hh), q, jnp.zeros((), BF16))
            work.append((vt_win, hh, _dot_t(k_win, qh) + bias_refs[sub][hh, 0], _dot_t(kc_ref[0], qh)))
    accs = []
    for vt_win, hh, s_win, s_ctx in work:
        m = jnp.maximum(jnp.max(s_win, axis=0, keepdims=True), jnp.max(s_ctx, axis=0, keepdims=True))
        p_win = jnp.exp2(s_win - m).astype(BF16)
        p_ctx = jnp.exp2(s_ctx - m).astype(BF16)
        accs.append(_dot(_with_ones_row(vt_win, hh), p_win) + _dot(_with_ones_row(vtc_ref[0], hh), p_ctx))
    for sub in range(NA_SUBS):
        o_ref[0, pl.ds(sub * NA_TQ, NA_TQ), :] = _finish_t(accs[2 * sub:2 * sub + 2])


def _na_bias(rel_bias):
    h, n_dr, n_dc = rel_bias.shape
    n_rows = 64
    w2 = 2 * GRID_W
    pad_lo = GRID_W - WIN_W
    p = jnp.pad(rel_bias * LOG2E, ((0, 0), (0, 0), (pad_lo, w2 - pad_lo - n_dc)))
    skew = jnp.broadcast_to(p[:, :, None, :], (h, n_dr, GRID_W, w2)).reshape(h, n_dr, GRID_W * w2)
    skew = skew[:, :, :GRID_W * (w2 - 1)].reshape(h, n_dr, GRID_W, w2 - 1)
    tiles = skew[..., GRID_W - 1:]
    qc = np.arange(GRID_W)
    cs = np.clip(qc - WIN_W // 2, 0, GRID_W - WIN_W)
    in_col = (qc[None, :] >= cs[:, None]) & (qc[None, :] < cs[:, None] + WIN_W)
    tiles = jnp.where(in_col, tiles, NEG_BIG)
    tiles = jnp.concatenate([tiles, jnp.full((h, 1, GRID_W, GRID_W), NEG_BIG, F32)], axis=1)
    ids = np.full((3, NA_KROWS, NA_QROWS), n_dr, np.int32)
    for v, t in enumerate((0, 1, n_rows // NA_QROWS - 1)):
        kb = int(np.clip(NA_QROWS * t - WIN_H // 2, 0, n_rows - NA_KROWS))
        for i in range(NA_QROWS):
            qr = NA_QROWS * t + i
            rs = int(np.clip(qr - WIN_H // 2, 0, n_rows - WIN_H))
            for kr in range(rs, rs + WIN_H):
                ids[v, kr - kb, i] = kr - qr + WIN_H - 1
    tiles_t = jnp.swapaxes(tiles, -1, -2)
    n_v = ids.shape[0]
    return pl.pallas_call(
        _na_bias_kernel,
        grid_spec=pltpu.PrefetchScalarGridSpec(
            num_scalar_prefetch=1,
            grid=(h, n_v),
            in_specs=[pl.BlockSpec((1, n_dr + 1, GRID_W, GRID_W), lambda hh, v, ids_ref: (hh, 0, 0, 0))],
            out_specs=pl.BlockSpec((1, 1, NA_KROWS * GRID_W, NA_TQ), lambda hh, v, ids_ref: (hh, v, 0, 0))),
        out_shape=jax.ShapeDtypeStruct((h, n_v, NA_KROWS * GRID_W, NA_TQ), F32),
        compiler_params=_params(2),
        name="na_bias",
    )(jnp.asarray(ids.reshape(-1)), tiles_t)


def _na_bias_kernel(ids_ref, tiles_ref, o_ref):
    v = pl.program_id(1)
    for kr in range(NA_KROWS):
        for i in range(0, NA_QROWS, 2):
            base = (v * NA_KROWS + kr) * NA_QROWS + i
            pair = jnp.concatenate([tiles_ref[0, ids_ref[base]], tiles_ref[0, ids_ref[base + 1]]], axis=-1)
            o_ref[0, 0, pl.ds(GRID_W * kr, GRID_W), pl.ds(GRID_W * i, 2 * GRID_W)] = pair


def _na_attention(q, k, vt, k_ctx, vt_ctx, bias):
    b, n, _ = q.shape
    step_q = NA_SUBS * NA_TQ
    nj = n // step_q
    n_sub = n // NA_TQ
    n_kblk = n // NA_KBLK
    n_ctx = k_ctx.shape[1]
    rows_per_blk = NA_KBLK // GRID_W

    def kstart(j, sub):
        t = NA_SUBS * j + sub
        return jnp.clip((NA_QROWS * t - WIN_H // 2) // rows_per_blk, 0, n_kblk - NA_KPIECES)

    def kspec(sub, i):
        return pl.BlockSpec((1, NA_KBLK, LANES), lambda p, j, bb: (bb, kstart(j, sub) + i, p))

    def vspec(sub, i):
        return pl.BlockSpec((1, LANES, NA_KBLK), lambda p, j, bb: (bb, p, kstart(j, sub) + i))

    def bspec(sub):
        def variant(j):
            t = NA_SUBS * j + sub
            return jnp.where(t == 0, 0, jnp.where(t == n_sub - 1, 2, 1))
        return pl.BlockSpec((2, 1, NA_KROWS * GRID_W, NA_TQ), lambda p, j, bb: (p, variant(j), 0, 0))

    subs_pieces = [(sub, i) for sub in range(NA_SUBS) for i in range(NA_KPIECES)]
    return pl.pallas_call(
        _na_kernel,
        grid=(2, nj, b),
        in_specs=[pl.BlockSpec((1, step_q, LANES), lambda p, j, bb: (bb, j, p))]
        + [kspec(sub, i) for sub, i in subs_pieces] + [vspec(sub, i) for sub, i in subs_pieces]
        + [pl.BlockSpec((1, n_ctx, LANES), lambda p, j, bb: (bb, 0, p)),
           pl.BlockSpec((1, LANES, n_ctx), lambda p, j, bb: (bb, p, 0))]
        + [bspec(sub) for sub in range(NA_SUBS)],
        out_specs=pl.BlockSpec((1, step_q, LANES), lambda p, j, bb: (bb, j, p)),
        out_shape=jax.ShapeDtypeStruct((b, n, 256), BF16),
        compiler_params=_params(3),
        name="na_lat",
    )(q, *([k] * len(subs_pieces)), *([vt] * len(subs_pieces)), k_ctx, vt_ctx, *([bias] * NA_SUBS))


LRU_SEG = 8
LRU_PAD = 8
LRU_ROWS = 512
LRU_HALVES = 2


def _pitch(n):
    seg = n // LRU_SEG
    return seg + 8 if (seg // 8) % 2 == 0 else seg


def _pieces(n, t0, r):
    seg, out, t = n // LRU_SEG, [], t0
    while t < t0 + r:
        length = min(seg - t % seg, t0 + r - t)
        out.append((t - t0, (t // seg) * _pitch(n) + t % seg, length))
        t += length
    return out


def _put(ref, n, t0, val):
    for off, row, length in _pieces(n, t0, val.shape[0]):
        for hv in range(LRU_HALVES):
            ref[hv, pl.ds(row, length), :] = val[off:off + length, LANES * hv:LANES * (hv + 1)]


def _get(ref, n, t0, r):
    return jnp.concatenate(
        [jnp.concatenate([ref[hv, pl.ds(row, length), :] for hv in range(LRU_HALVES)], axis=-1)
         for _, row, length in _pieces(n, t0, r)], axis=0)


def _lru_gates(xp, n, cw, cb, wa_ref, ba_ref, wx_ref, bx_ref, sp, a_refs, u_refs):
    r = min(LRU_ROWS, n)
    for c0 in range(0, n, r):
        conv = cb
        for tap in range(CONV_W):
            conv = conv + cw[tap:tap + 1, :] * xp[pl.ds(LRU_PAD + c0 + tap - CONV_W // 2, r), :]
        xb = conv.astype(BF16)
        for d in range(2):
            rg = _sigmoid(_dot(xb, wa_ref[d]) + ba_ref[d:d + 1, :])
            ig = _sigmoid(_dot(xb, wx_ref[d]) + bx_ref[d:d + 1, :])
            a = jnp.exp((-LRU_C) * rg * sp[d:d + 1, :])
            _put(a_refs[d], n, c0, a)
            _put(u_refs[d], n, c0, jnp.sqrt(1.0 - a * a) * (ig * conv))


def _lru_scan(n, af, uf, ab, ub, hf0, hb0):
    seg, pitch = n // LRU_SEG, _pitch(n)

    def body(i, carry):
        rf = pl.ds(i, LRU_SEG, stride=pitch)
        rb = pl.ds(seg - 1 - i, LRU_SEG, stride=pitch)
        out = []
        for hv in range(LRU_HALVES):
            hlf, cmf, hlb, cmb = carry[4 * hv:4 * hv + 4]
            a1, u1 = af[hv, rf, :], uf[hv, rf, :]
            a2, u2 = ab[hv, rb, :], ub[hv, rb, :]
            hlf = a1 * hlf + u1
            cmf = a1 * cmf
            hlb = a2 * hlb + u2
            cmb = a2 * cmb
            uf[hv, rf, :] = hlf
            af[hv, rf, :] = cmf
            ub[hv, rb, :] = hlb
            ab[hv, rb, :] = cmb
            out += [hlf, cmf, hlb, cmb]
        return tuple(out)

    zero = jnp.zeros((LRU_SEG, LANES), F32)
    one = jnp.ones((LRU_SEG, LANES), F32)
    lax.fori_loop(0, seg, body, (zero, one, zero, one) * LRU_HALVES)
    hf, hb = list(hf0), list(hb0)
    for s in range(LRU_SEG):
        rf = pl.ds(s * pitch, seg)
        rb = pl.ds((LRU_SEG - 1 - s) * pitch, seg)
        for hv in range(LRU_HALVES):
            h = uf[hv, rf, :] + af[hv, rf, :] * hf[hv]
            uf[hv, rf, :] = h
            hf[hv] = h[seg - 1:seg, :]
            h = ub[hv, rb, :] + ab[hv, rb, :] * hb[hv]
            ub[hv, rb, :] = h
            hb[hv] = h[0:1, :]
    return hf, hb


def _lru_kernel(xl_ref, xc_ref, cw_ref, cb_ref, wa_ref, ba_ref, wx_ref, bx_ref, lam_ref,
                yl_ref, yc_ref, xp, afl, ufl, abl, ubl, afc, ufc, abc, ubc):
    n_lat, w = xl_ref.shape[1], xl_ref.shape[2]
    n_ctx = xc_ref.shape[1]
    cw, cb = cw_ref[...], cb_ref[...]
    neg_lam = -lam_ref[...]
    sp = jnp.maximum(neg_lam, 0.0) + jnp.log1p(jnp.exp(-jnp.abs(neg_lam)))
    zpad = jnp.zeros((LRU_PAD, w), F32)

    xp[pl.ds(0, LRU_PAD), :] = zpad
    xp[pl.ds(LRU_PAD, n_ctx), :] = xc_ref[0]
    xp[pl.ds(LRU_PAD + n_ctx, LRU_PAD), :] = zpad
    _lru_gates(xp, n_ctx, cw, cb, wa_ref, ba_ref, wx_ref, bx_ref, sp, (afc, abc), (ufc, ubc))
    xp[pl.ds(LRU_PAD, n_lat), :] = xl_ref[0]
    xp[pl.ds(LRU_PAD + n_lat, LRU_PAD), :] = zpad
    _lru_gates(xp, n_lat, cw, cb, wa_ref, ba_ref, wx_ref, bx_ref, sp, (afl, abl), (ufl, ubl))

    h0 = [jnp.zeros((1, LANES), F32)] * LRU_HALVES
    hf, hb = _lru_scan(n_ctx, afc, ufc, abc, ubc, h0, h0)
    _lru_scan(n_lat, afl, ufl, abl, ubl, hf, hb)
    yc_ref[0] = (_get(ufc, n_ctx, 0, n_ctx) + _get(ubc, n_ctx, 0, n_ctx)).astype(BF16)
    for c0 in range(0, n_lat, LRU_ROWS):
        yl_ref[0, pl.ds(c0, LRU_ROWS), :] = (_get(ufl, n_lat, c0, LRU_ROWS)
                                             + _get(ubl, n_lat, c0, LRU_ROWS)).astype(BF16)


def _lru(x_lat, x_ctx, conv_w, conv_b, wa, ba, wx, bx, lam):
    b, n_lat, w = x_lat.shape
    n_ctx = x_ctx.shape[1]
    consts = (conv_w, conv_b, wa, ba, wx, bx, lam)
    big = lambda: pltpu.VMEM((LRU_HALVES, LRU_SEG * _pitch(n_lat), LANES), F32)
    small = lambda: pltpu.VMEM((LRU_HALVES, LRU_SEG * _pitch(n_ctx), LANES), F32)
    return pl.pallas_call(
        _lru_kernel,
        grid=(b,),
        in_specs=[pl.BlockSpec((1, n_lat, w), lambda bb: (bb, 0, 0)),
                  pl.BlockSpec((1, n_ctx, w), lambda bb: (bb, 0, 0))]
        + [_const_spec(c.shape) for c in consts],
        out_specs=[pl.BlockSpec((1, n_lat, w), lambda bb: (bb, 0, 0)),
                   pl.BlockSpec((1, n_ctx, w), lambda bb: (bb, 0, 0))],
        out_shape=[jax.ShapeDtypeStruct((b, n_lat, w), BF16),
                   jax.ShapeDtypeStruct((b, n_ctx, w), BF16)],
        scratch_shapes=[pltpu.VMEM((n_lat + 2 * LRU_PAD, w), F32),
                        big(), big(), big(), big(), small(), small(), small(), small()],
        compiler_params=_params(1),
        name="rglru",
    )(x_lat, x_ctx, *consts)


MERGE_SUB_ROWS = 256


def _merge_kernel(x_ref, mod_ref, ya_ref, yb_ref, yc_ref, yd_ref, wz_ref, wm_ref, wb_ref, wo_ref,
                  lng_ref, lnb_ref, o_ref):
    shift = mod_ref[0, 0:1, :]
    scale = mod_ref[0, 1:2, :]
    gate = mod_ref[0, 2:3, :]
    tile = x_ref.shape[1]
    sub = min(tile, MERGE_SUB_ROWS)
    for r0 in range(0, tile, sub):
        rows = pl.ds(r0, sub)
        x = x_ref[0, rows, :]
        xm = (_layer_norm(x) * (1.0 + scale) + shift).astype(BF16)
        z = _dot(xm, wz_ref[...])
        acc = None
        for i, y_ref in enumerate((ya_ref, yb_ref, yc_ref, yd_ref)):
            zi = z[:, BRANCH_W * i:BRANCH_W * (i + 1)]
            g = (y_ref[0, rows, :].astype(F32) * (zi * _sigmoid(zi))).astype(BF16)
            t = _dot(g, wb_ref[i])
            mi = _dot(xm, wm_ref[:, D_MODEL * i:D_MODEL * (i + 1)])
            term = _sigmoid(mi) * t
            acc = term if acc is None else acc + term
        out = _dot(acc.astype(BF16), wo_ref[...])
        o_ref[0, rows, :] = _layer_norm(DEEPNORM_ALPHA * x + gate * out) * lng_ref[...] + lnb_ref[...]


def _merge(x, mod, ys, wz, wm, wb, wo, ln_g, ln_b, tile):
    b, n, _ = x.shape
    consts = (wz, wm, wb, wo, ln_g, ln_b)
    tok = lambda w: pl.BlockSpec((1, tile, w), lambda bb, t: (bb, t, 0))
    return pl.pallas_call(
        _merge_kernel,
        grid=(b, n // tile),
        in_specs=[tok(D_MODEL), pl.BlockSpec((1, 3, D_MODEL), lambda bb, t: (bb, 0, 0))]
        + [tok(BRANCH_W) for _ in ys] + [_const_spec(c.shape, single_buffer=True) for c in consts],
        out_specs=tok(D_MODEL),
        out_shape=jax.ShapeDtypeStruct((b, n, D_MODEL), F32),
        compiler_params=_params(2),
        name="merge",
    )(x, mod, *ys, *consts)


def _rope_tables(n):
    t = jnp.arange(n, dtype=jnp.int32)
    rows = (t // GRID_W).astype(F32)[:, None]
    cols = (t % GRID_W).astype(F32)[:, None]

    def table(half, offsets):
        inv = ROPE_THETA ** (-np.arange(0, 2 * half, 2, dtype=np.float64) / (2 * half))
        f_row, f_col, sign = (np.zeros(LANES, np.float32) for _ in range(3))
        for offset in offsets:
            for g, f in enumerate((f_row, f_col)):
                base = offset + 2 * half * g
                f[base:base + 2 * half] = np.concatenate([inv, inv])
                sign[base:base + 2 * half] = np.concatenate([-np.ones(half), np.ones(half)])
        ang = rows * f_row[None, :] + cols * f_col[None, :]
        return jnp.cos(ang), jnp.sin(ang) * sign[None, :]

    cosd, sind = table(16, (0, HEAD_DIM))
    cosb, sinb = table(8, (MLA_NOPE,))
    cosk, sink = table(8, (0,))
    return cosd, sind, cosb, sinb, cosk, sink


def _identity_tables(n):
    one, zero = jnp.ones((n, LANES), F32), jnp.zeros((n, LANES), F32)
    return one, zero, one, zero, one, zero


def _layer_weights(w_in, mla_q_norm, mla_w_uq, mla_kv_norm, mla_w_ukv, gqa_q_norm, gqa_k_norm, w_branch, w_out):
    w_in = w_in.astype(BF16)
    wi = w_in[:, :MIX_COLS]
    qa, ka, va = wi[:, 0:256], wi[:, 256:512], wi[:, 512:768]
    cq, ckv, kr = wi[:, 768:1024], wi[:, 1024:1152], wi[:, 1152:1184]
    lru, qd, kd, vd = wi[:, 1184:1440], wi[:, 1440:1696], wi[:, 1696:1824], wi[:, 1824:1952]
    head_order = jnp.array([0, 2, 1, 3])
    qd = qd.reshape(D_MODEL, 4, HEAD_DIM)[:, head_order].reshape(D_MODEL, 256)
    wmix = jnp.concatenate([qa, ka, va, cq, ckv, lru, qd, kd, vd, kr,
                            jnp.zeros((D_MODEL, MIX_PAD - MIX_COLS), BF16)], -1)
    hq = MLA_NOPE + MLA_ROPE
    wuq = jnp.pad(mla_w_uq.reshape(256, MLA_HEADS, hq), ((0, 0), (0, 0), (0, LANES - hq)))
    wuq = wuq.reshape(256, MLA_HEADS * LANES).astype(BF16)
    ukv = mla_w_ukv.reshape(128, MLA_HEADS, 128)
    wuk = jnp.pad(ukv[:, :, :MLA_NOPE], ((0, 0), (0, 0), (0, LANES - MLA_NOPE)))
    wuk = wuk.reshape(128, MLA_HEADS * LANES).astype(BF16)
    wuv = ukv[:, :, MLA_NOPE:].reshape(128, 256).astype(BF16)
    pk = np.zeros((LANES, MLA_HEADS * LANES), np.float32)
    for h in range(MLA_HEADS):
        pk[np.arange(MLA_ROPE), LANES * h + MLA_NOPE + np.arange(MLA_ROPE)] = 1.0
    grp = np.kron(np.eye(4, dtype=np.float32), np.ones((HEAD_DIM, HEAD_DIM), np.float32))
    inproj_w = (wmix, wuq, wuk, wuv, jnp.asarray(pk, BF16), jnp.asarray(grp, BF16),
                mla_q_norm.reshape(1, 256), mla_kv_norm.reshape(1, 128),
                jnp.tile(gqa_q_norm, 4).reshape(1, 256), jnp.tile(gqa_k_norm, 2).reshape(1, 128))
    wz = w_in[:, MIX_COLS:MIX_COLS + SILU_COLS]
    wz_d = wz[:, 3 * BRANCH_W:].reshape(D_MODEL, 4, HEAD_DIM)[:, head_order].reshape(D_MODEL, BRANCH_W)
    wz = jnp.concatenate([wz[:, :3 * BRANCH_W], wz_d], -1)
    wm = w_in[:, MIX_COLS + SILU_COLS:]
    wb_d = w_branch[3].reshape(4, HEAD_DIM, D_MODEL)[head_order].reshape(BRANCH_W, D_MODEL)
    wb = jnp.concatenate([w_branch[:3], wb_d[None]], 0).astype(BF16)
    return inproj_w, (wz, wm, wb, w_out.astype(BF16))


def _block_diag(w):
    eye = jnp.eye(4, dtype=w.dtype)
    return jnp.einsum("dkce,kj->dkcje", w, eye).reshape(2, 256, 256)


def kernel(x, c, ctx, c_ctx, w_mod, b_mod, w_in, na_rel_bias, mla_q_norm, mla_w_uq, mla_kv_norm, mla_w_ukv,
           lru_conv_w, lru_conv_b, lru_w_a, lru_b_a, lru_w_x, lru_b_x, lru_lambda, gqa_q_norm, gqa_k_norm,
           w_branch, w_out, ln_g, ln_b):
    b, n, _ = x.shape
    n_ctx = ctx.shape[1]
    rows = 8 * ((b + 1 + 7) // 8)
    c_all = jnp.zeros((rows, D_MODEL), F32).at[:b].set(c).at[b].set(c_ctx)
    mod_all = _modulation(c_all, w_mod, b_mod)
    lat_tables = _rope_tables(n)
    ctx_tables = _identity_tables(n_ctx)

    for l in range(DEPTH):
        need_ctx = l < DEPTH - 1
        mod_lat = mod_all[l, :b].reshape(b, 3, D_MODEL)
        mod_ctx = jnp.broadcast_to(mod_all[l, b].reshape(1, 3, D_MODEL), (b, 3, D_MODEL))
        inproj_w, merge_w = _layer_weights(w_in[l], mla_q_norm[l], mla_w_uq[l], mla_kv_norm[l], mla_w_ukv[l],
                                           gqa_q_norm[l], gqa_k_norm[l], w_branch[l], w_out[l])
        qa, ka, va, qb, kb, vb, xr, qd, kd, vd, vat, vbt, vdt = _inproj(x, mod_lat, inproj_w, lat_tables, 1024)
        qa_c, ka_c, va_c, qb_c, kb_c, vb_c, xr_c, qd_c, kd_c, vd_c, vat_c, vbt_c, vdt_c = _inproj(
            ctx, mod_ctx, inproj_w, ctx_tables, n_ctx)

        ya = _na_attention(qa, ka, vat, ka_c, vat_c, _na_bias(na_rel_bias[l]))
        yb = _attention_t(qb, kb, vbt, kb_c, vbt_c, mode="B", tq=2048)
        yd = _attention_t(qd, kd, vdt, kd_c, vdt_c, mode="D", tq=2048)
        yc, yc_c = _lru(xr, xr_c, lru_conv_w[l], lru_conv_b[l].reshape(1, -1),
                        _block_diag(lru_w_a[l]).astype(BF16), lru_b_a[l],
                        _block_diag(lru_w_x[l]).astype(BF16), lru_b_x[l], lru_lambda[l])
        tail = (*merge_w, ln_g[l].reshape(1, -1), ln_b[l].reshape(1, -1))
        x_new = _merge(x, mod_lat, (ya, yb, yc, yd), *tail, 1024)
        if need_ctx:
            ya_c = _attention(qa_c, None, None, ka_c, va_c, mode="A", tq=n_ctx)
            yb_c = _attention(qb_c, None, None, kb_c, vb_c, mode="B", tq=n_ctx)
            yd_c = _attention(qd_c, None, None, kd_c, vd_c, mode="D", tq=n_ctx)
            ctx = _merge(ctx, mod_ctx, (ya_c, yb_c, yc_c, yd_c), *tail, n_ctx)
        x = x_new
    return x
```

```python
import functools
import math

import numpy as np
import jax
import jax.numpy as jnp
from jax import lax
from jax.experimental import pallas as pl
from jax.experimental.pallas import tpu as pltpu

F32 = jnp.float32
BF16 = jnp.bfloat16

D_MODEL = 1024
DEPTH = 2
GRID_W = 64
BRANCH_W = 256
HEAD_DIM = 64
WIN_H = 8
WIN_W = 16
MLA_HEADS = 4
MLA_NOPE = 64
MLA_ROPE = 32
LRU_C = 8.0
CONV_W = 4
ROPE_THETA = 10000.0
EPS = 1e-6
LOG2E = 1.4426950408889634
NA_SCALE = HEAD_DIM ** -0.5
MLA_SCALE = (MLA_NOPE + MLA_ROPE) ** -0.5
GQA_SCALE = HEAD_DIM ** -0.5
DEEPNORM_ALPHA = (2 * DEPTH) ** 0.25

MIX_COLS = 1952
SILU_COLS = 1024
LANES = 128
NEG_BIG = -1e30
VMEM_LIMIT = 56 * 1024 * 1024

OFF_QA, OFF_KA, OFF_VA, OFF_CQ, OFF_CKV, OFF_LRU, OFF_QD, OFF_KD, OFF_VD, OFF_KR = (
    0, 256, 512, 768, 1024, 1152, 1408, 1664, 1792, 1920)
MIX_PAD = 2048


def _dot(a, b):
    return jnp.dot(a, b, preferred_element_type=F32)


def _dot_t(a, b):
    return lax.dot_general(a, b, (((1,), (1,)), ((), ())), preferred_element_type=F32)


def _layer_norm(x):
    mu = jnp.mean(x, axis=-1, keepdims=True)
    xc = x - mu
    var = jnp.mean(xc * xc, axis=-1, keepdims=True)
    return xc * lax.rsqrt(var + EPS)


def _sigmoid(x):
    return 0.5 * jnp.tanh(0.5 * x) + 0.5


def _rms(x, g):
    return x * lax.rsqrt(jnp.mean(x * x, axis=-1, keepdims=True) + EPS) * g


def _params(n_grid):
    return pltpu.CompilerParams(dimension_semantics=("arbitrary",) * n_grid,
                                vmem_limit_bytes=VMEM_LIMIT)


def _const_spec(shape, single_buffer=False):
    zeros = (0,) * len(shape)
    mode = pl.Buffered(1) if single_buffer else None
    return pl.BlockSpec(shape, lambda *_: zeros, pipeline_mode=mode)


def _mod_kernel(c_ref, w_ref, b_ref, o_ref):
    c = c_ref[...]
    s = (c * jax.nn.sigmoid(c)).astype(BF16)
    o_ref[0] = _dot(s, w_ref[0].astype(BF16)) + b_ref[0]


def _modulation(c_all, w_mod, b_mod):
    n_l, _, n_out = w_mod.shape
    rows = c_all.shape[0]
    bn = 512
    return pl.pallas_call(
        _mod_kernel,
        grid=(n_l, n_out // bn),
        in_specs=[pl.BlockSpec((rows, D_MODEL), lambda l, n: (0, 0)),
                  pl.BlockSpec((1, D_MODEL, bn), lambda l, n: (l, 0, n)),
                  pl.BlockSpec((1, 1, bn), lambda l, n: (l, 0, n))],
        out_specs=pl.BlockSpec((1, rows, bn), lambda l, n: (l, 0, n)),
        out_shape=jax.ShapeDtypeStruct((n_l, rows, n_out), F32),
        compiler_params=_params(2),
        name="modulation",
    )(c_all, w_mod, b_mod.reshape(n_l, 1, n_out))


def _pair_select(lane, shift):
    lane_up = pltpu.roll(lane, LANES - shift, 1)
    want = jnp.where((lane & (2 * shift - 1)) < shift, lane + shift, lane - shift)
    return lane_up == want


def _rope(x, cos, sin, shift, sel):
    partner = jnp.where(sel, pltpu.roll(x, LANES - shift, 1), pltpu.roll(x, shift, 1))
    return x * cos + partner * sin


def _group_sumsq(x, g):
    sq = x * x
    hi = sq.astype(BF16)
    lo = (sq - hi.astype(F32)).astype(BF16)
    return _dot(hi, g) + _dot(lo, g)


INPROJ_SUB_ROWS = 512


def _inproj_kernel(x_ref, mod_ref, wmix_ref, wuq_ref, wuk_ref, wuv_ref, pk_ref, g_ref,
                   gq_ref, gkv_ref, gdq_ref, gdk_ref,
                   cosd_ref, sind_ref, cosb_ref, sinb_ref, cosk_ref, sink_ref,
                   qa_o, ka_o, va_o, qb_o, kb_o, vb_o, xc_o, qd_o, kd_o, vd_o, vat_o, vbt_o, vdt_o):
    shift = mod_ref[0, 0:1, :]
    scale = mod_ref[0, 1:2, :]
    tile = x_ref.shape[1]
    sub = min(tile, INPROJ_SUB_ROWS)
    lane = lax.broadcasted_iota(jnp.int32, (sub, LANES), 1)
    sel16 = _pair_select(lane, 16)
    sel8 = _pair_select(lane, 8)
    for r0 in range(0, tile, sub):
        rows = pl.ds(r0, sub)
        xm = (_layer_norm(x_ref[0, rows, :]) * (1.0 + scale) + shift).astype(BF16)
        p = _dot(xm, wmix_ref[...])

        qa_o[0, rows, :] = (p[:, OFF_QA:OFF_QA + 256] * (NA_SCALE * LOG2E)).astype(BF16)
        ka_o[0, rows, :] = p[:, OFF_KA:OFF_KA + 256].astype(BF16)
        va = p[:, OFF_VA:OFF_VA + 256]
        va_o[0, rows, :] = va.astype(BF16)
        vat_o[0, :, rows] = va.T.astype(BF16)

        cqn = _rms(p[:, OFF_CQ:OFF_CQ + 256], gq_ref[...]).astype(BF16)
        qb = _dot(cqn, wuq_ref[...])
        cosb, sinb = cosb_ref[rows, :], sinb_ref[rows, :]
        for h in range(MLA_HEADS):
            blk = _rope(qb[:, LANES * h:LANES * (h + 1)], cosb, sinb, 8, sel8)
            qb_o[0, rows, LANES * h:LANES * (h + 1)] = (blk * (MLA_SCALE * LOG2E)).astype(BF16)
        ckvn = _rms(p[:, OFF_CKV:OFF_CKV + 128], gkv_ref[...]).astype(BF16)
        kr = _rope(p[:, OFF_KR:OFF_KR + 128], cosk_ref[rows, :], sink_ref[rows, :], 8, sel8).astype(BF16)
        kb_o[0, rows, :] = (_dot(ckvn, wuk_ref[...]) + _dot(kr, pk_ref[...])).astype(BF16)
        vb = _dot(ckvn, wuv_ref[...])
        vb_o[0, rows, :] = vb.astype(BF16)
        vbt_o[0, :, rows] = vb.T.astype(BF16)

        xc_o[0, rows, :] = p[:, OFF_LRU:OFF_LRU + 256]

        cosd, sind = cosd_ref[rows, :], sind_ref[rows, :]
        qd = p[:, OFF_QD:OFF_QD + 256]
        qd = qd * lax.rsqrt(_group_sumsq(qd, g_ref[...]) * (1.0 / HEAD_DIM) + EPS) * gdq_ref[...]
        for blk_i in range(2):
            blk = _rope(qd[:, LANES * blk_i:LANES * (blk_i + 1)], cosd, sind, 16, sel16)
            qd_o[0, rows, LANES * blk_i:LANES * (blk_i + 1)] = (blk * (GQA_SCALE * LOG2E)).astype(BF16)
        kd = p[:, OFF_KD:OFF_KD + 128]
        kd = kd * lax.rsqrt(_group_sumsq(kd, g_ref[0:128, 0:128]) * (1.0 / HEAD_DIM) + EPS) * gdk_ref[...]
        kd_o[0, rows, :] = _rope(kd, cosd, sind, 16, sel16).astype(BF16)
        vd = p[:, OFF_VD:OFF_VD + 128]
        vd_o[0, rows, :] = vd.astype(BF16)
        vdt_o[0, :, rows] = vd.T.astype(BF16)


def _inproj(x, mod, weights, tables, tile):
    b, n, _ = x.shape
    nt = n // tile
    wspecs = [_const_spec(w.shape) for w in weights]
    tspecs = [pl.BlockSpec((tile, LANES), lambda t, bb: (t, 0)) for _ in tables]
    widths = (256, 256, 256, 512, 512, 256, 256, 256, 128, 128)
    dtypes = (BF16, BF16, BF16, BF16, BF16, BF16, F32, BF16, BF16, BF16)
    t_rows = (256, 256, 128)
    return pl.pallas_call(
        _inproj_kernel,
        grid=(nt, b),
        in_specs=[pl.BlockSpec((1, tile, D_MODEL), lambda t, bb: (bb, t, 0)),
                  pl.BlockSpec((1, 3, D_MODEL), lambda t, bb: (bb, 0, 0))] + wspecs + tspecs,
        out_specs=[pl.BlockSpec((1, tile, w), lambda t, bb: (bb, t, 0)) for w in widths]
        + [pl.BlockSpec((1, r, tile), lambda t, bb: (bb, 0, t)) for r in t_rows],
        out_shape=[jax.ShapeDtypeStruct((b, n, w), dt) for w, dt in zip(widths, dtypes)]
        + [jax.ShapeDtypeStruct((b, r, n), BF16) for r in t_rows],
        compiler_params=_params(2),
        name="inproj",
    )(x, mod, *weights, *tables)


def _half_mask(lane, hh):
    return lane >= HEAD_DIM if hh else lane < HEAD_DIM


def _chunk_update(qh, kc, vc, m, l, acc):
    s = _dot_t(qh, kc)
    m_new = jnp.maximum(m, jnp.max(s, axis=-1, keepdims=True))
    alpha = jnp.exp2(m - m_new)
    pr = jnp.exp2(s - m_new)
    l = alpha * l + jnp.sum(pr, axis=-1, keepdims=True)
    acc = alpha * acc + _dot(pr.astype(BF16), vc)
    return m_new, l, acc


def _attn_kernel(*refs, masked, n_lat, chunk):
    if n_lat:
        q_ref, kl_ref, vl_ref, kc_ref, vc_ref, o_ref = refs
    else:
        q_ref, kc_ref, vc_ref, o_ref = refs
    tq = q_ref.shape[1]
    lane = lax.broadcasted_iota(jnp.int32, (tq, LANES), 1)
    qs, ksls = [], []
    for hh in range(2):
        if masked:
            qs.append(jnp.where(_half_mask(lane, hh), q_ref[0], jnp.zeros((), BF16)))
            ksls.append(slice(None))
        else:
            qs.append(q_ref[0, :, LANES * hh:LANES * (hh + 1)])
            ksls.append(slice(LANES * hh, LANES * (hh + 1)))
    carries = [(jnp.full((tq, 1), NEG_BIG, F32), jnp.zeros((tq, 1), F32), jnp.zeros((tq, LANES), F32))
               for _ in range(2)]
    for c0 in range(0, n_lat, chunk):
        rows = pl.ds(c0, chunk)
        for hh in range(2):
            carries[hh] = _chunk_update(qs[hh], kl_ref[0, rows, ksls[hh]], vl_ref[0, rows, :], *carries[hh])
    outs = []
    for hh in range(2):
        m, l, acc = _chunk_update(qs[hh], kc_ref[0, :, ksls[hh]], vc_ref[0], *carries[hh])
        outs.append(acc * (1.0 / l))
    o_ref[0] = jnp.where(lane < HEAD_DIM, outs[0], outs[1]).astype(BF16)


def _attention(q, k_lat, v_lat, k_ctx, v_ctx, *, mode, tq, chunk=1024):
    b, nq, _ = q.shape
    masked = mode != "B"
    qw = 256 if mode == "B" else LANES
    kw = 256 if mode == "B" else LANES
    kidx = (lambda bb, p, t: (bb, 0, 0)) if mode == "D" else (lambda bb, p, t: (bb, 0, p))
    n_ctx = k_ctx.shape[1]
    n_lat = 0 if k_lat is None else k_lat.shape[1]
    in_specs = [pl.BlockSpec((1, tq, qw), lambda bb, p, t: (bb, t, p))]
    args = [q]
    if n_lat:
        in_specs += [pl.BlockSpec((1, n_lat, kw), kidx), pl.BlockSpec((1, n_lat, LANES), kidx)]
        args += [k_lat, v_lat]
    in_specs += [pl.BlockSpec((1, n_ctx, kw), kidx), pl.BlockSpec((1, n_ctx, LANES), kidx)]
    args += [k_ctx, v_ctx]
    return pl.pallas_call(
        functools.partial(_attn_kernel, masked=masked, n_lat=n_lat, chunk=chunk),
        grid=(b, 2, nq // tq),
        in_specs=in_specs,
        out_specs=pl.BlockSpec((1, tq, LANES), lambda bb, p, t: (bb, t, p)),
        out_shape=jax.ShapeDtypeStruct((b, nq, 256), BF16),
        compiler_params=_params(3),
        name="attn_" + mode + ("_lat" if n_lat else "_ctx"),
    )(*args)


def _with_ones_row(vt, hh):
    row = lax.broadcasted_iota(jnp.int32, vt.shape, 0)
    return jnp.where(row == _ones_row(hh), jnp.ones((), BF16), vt)


def _ones_row(hh):
    return 0 if hh else HEAD_DIM


def _softmax_update_t(s, vtc, m, acc):
    m_new = jnp.maximum(m, jnp.max(s, axis=0, keepdims=True))
    alpha = jnp.exp2(m - m_new)
    pr = jnp.exp2(s - m_new)
    acc = alpha * acc + _dot(vtc, pr.astype(BF16))
    return m_new, acc


def _finish_t(accs):
    outs = [acc * (1.0 / acc[_ones_row(hh):_ones_row(hh) + 1, :]) for hh, acc in enumerate(accs)]
    row = lax.broadcasted_iota(jnp.int32, outs[0].shape, 0)
    return jnp.where(row < HEAD_DIM, outs[0], outs[1]).T.astype(BF16)


def _attn_t_kernel(q_ref, kl_ref, vtl_ref, kc_ref, vtc_ref, o_ref, *, masked, chunk, ahead):
    tq = q_ref.shape[1]
    n_lat = kl_ref.shape[1]
    lane = lax.broadcasted_iota(jnp.int32, (tq, LANES), 1)
    qs, ksls = [], []
    for hh in range(2):
        if masked:
            qs.append(jnp.where(_half_mask(lane, hh), q_ref[0], jnp.zeros((), BF16)))
            ksls.append(slice(None))
        else:
            qs.append(q_ref[0, :, LANES * hh:LANES * (hh + 1)])
            ksls.append(slice(LANES * hh, LANES * (hh + 1)))
    carries = [(jnp.full((1, tq), NEG_BIG, F32), jnp.zeros((LANES, tq), F32)) for _ in range(2)]
    tasks = [(hh, (kl_ref, pl.ds(c0, chunk)), (vtl_ref, pl.ds(c0, chunk)))
             for c0 in range(0, n_lat, chunk) for hh in range(2)]
    tasks += [(hh, (kc_ref, slice(None)), (vtc_ref, slice(None))) for hh in range(2)]

    def scores(task):
        hh, (k_ref, rows), _ = task
        return _dot_t(k_ref[0, rows, ksls[hh]], qs[hh])

    pending = [scores(task) for task in tasks[:ahead]]
    for i, task in enumerate(tasks):
        s_cur = pending.pop(0)
        if i + ahead < len(tasks):
            pending.append(scores(tasks[i + ahead]))
        hh, _, (vt_ref, cols) = task
        carries[hh] = _softmax_update_t(s_cur, _with_ones_row(vt_ref[0, :, cols], hh), *carries[hh])
    o_ref[0] = _finish_t([acc for _, acc in carries])


def _attention_t(q, k_lat, vt_lat, k_ctx, vt_ctx, *, mode, tq, ahead, chunk=512):
    b, nq, _ = q.shape
    masked = mode != "B"
    qw = 256 if mode == "B" else LANES
    kidx = (lambda bb, p, t: (bb, 0, 0)) if mode == "D" else (lambda bb, p, t: (bb, 0, p))
    vidx = (lambda bb, p, t: (bb, 0, 0)) if mode == "D" else (lambda bb, p, t: (bb, p, 0))
    n_lat, n_ctx = k_lat.shape[1], k_ctx.shape[1]
    return pl.pallas_call(
        functools.partial(_attn_t_kernel, masked=masked, chunk=chunk, ahead=ahead),
        grid=(b, 2, nq // tq),
        in_specs=[pl.BlockSpec((1, tq, qw), lambda bb, p, t: (bb, t, p)),
                  pl.BlockSpec((1, n_lat, qw), kidx), pl.BlockSpec((1, LANES, n_lat), vidx),
                  pl.BlockSpec((1, n_ctx, qw), kidx), pl.BlockSpec((1, LANES, n_ctx), vidx)],
        out_specs=pl.BlockSpec((1, tq, LANES), lambda bb, p, t: (bb, t, p)),
        out_shape=jax.ShapeDtypeStruct((b, nq, 256), BF16),
        compiler_params=_params(3),
        name="attn_" + mode + "_lat",
    )(q, k_lat, vt_lat, k_ctx, vt_ctx)


NA_QROWS = 4
NA_KROWS = 12
NA_TQ = NA_QROWS * GRID_W
NA_KBLK = 256
NA_KPIECES = NA_KROWS * GRID_W // NA_KBLK
NA_SUBS = 2


def _na_kernel(q_ref, *refs):
    k_refs, refs = refs[:NA_SUBS * NA_KPIECES], refs[NA_SUBS * NA_KPIECES:]
    vt_refs, refs = refs[:NA_SUBS * NA_KPIECES], refs[NA_SUBS * NA_KPIECES:]
    kc_ref, vtc_ref = refs[:2]
    bias_refs, o_ref = refs[2:2 + NA_SUBS], refs[2 + NA_SUBS]
    lane = lax.broadcasted_iota(jnp.int32, (NA_TQ, LANES), 1)
    work = []
    for sub in range(NA_SUBS):
        q = q_ref[0, pl.ds(sub * NA_TQ, NA_TQ), :]
        pieces = slice(sub * NA_KPIECES, (sub + 1) * NA_KPIECES)
        k_win = jnp.concatenate([r[0] for r in k_refs[pieces]], axis=0)
        vt_win = jnp.concatenate([r[0] for r in vt_refs[pieces]], axis=1)
        for hh in range(2):
            qh = jnp.where(_half_mask(lane, hh), q, jnp.zeros((), BF16))
            work.append((vt_win, hh, _dot_t(k_win, qh) + bias_refs[sub][hh, 0], _dot_t(kc_ref[0], qh)))
    accs = []
    for vt_win, hh, s_win, s_ctx in work:
        m = jnp.maximum(jnp.max(s_win, axis=0, keepdims=True), jnp.max(s_ctx, axis=0, keepdims=True))
        p_win = jnp.exp2(s_win - m).astype(BF16)
        p_ctx = jnp.exp2(s_ctx - m).astype(BF16)
        accs.append(_dot(_with_ones_row(vt_win, hh), p_win) + _dot(_with_ones_row(vtc_ref[0], hh), p_ctx))
    for sub in range(NA_SUBS):
        o_ref[0, pl.ds(sub * NA_TQ, NA_TQ), :] = _finish_t(accs[2 * sub:2 * sub + 2])


def _na_bias(rel_bias):
    h, n_dr, n_dc = rel_bias.shape
    n_rows = 64
    w2 = 2 * GRID_W
    pad_lo = GRID_W - WIN_W
    p = jnp.pad(rel_bias * LOG2E, ((0, 0), (0, 0), (pad_lo, w2 - pad_lo - n_dc)))
    skew = jnp.broadcast_to(p[:, :, None, :], (h, n_dr, GRID_W, w2)).reshape(h, n_dr, GRID_W * w2)
    skew = skew[:, :, :GRID_W * (w2 - 1)].reshape(h, n_dr, GRID_W, w2 - 1)
    tiles = skew[..., GRID_W - 1:]
    qc = np.arange(GRID_W)
    cs = np.clip(qc - WIN_W // 2, 0, GRID_W - WIN_W)
    in_col = (qc[None, :] >= cs[:, None]) & (qc[None, :] < cs[:, None] + WIN_W)
    tiles = jnp.where(in_col, tiles, NEG_BIG)
    tiles = jnp.concatenate([tiles, jnp.full((h, 1, GRID_W, GRID_W), NEG_BIG, F32)], axis=1)
    ids = np.full((3, NA_KROWS, NA_QROWS), n_dr, np.int32)
    for v, t in enumerate((0, 1, n_rows // NA_QROWS - 1)):
        kb = int(np.clip(NA_QROWS * t - WIN_H // 2, 0, n_rows - NA_KROWS))
        for i in range(NA_QROWS):
            qr = NA_QROWS * t + i
            rs = int(np.clip(qr - WIN_H // 2, 0, n_rows - WIN_H))
            for kr in range(rs, rs + WIN_H):
                ids[v, kr - kb, i] = kr - qr + WIN_H - 1
    tiles_t = jnp.swapaxes(tiles, -1, -2)
    n_v = ids.shape[0]
    return pl.pallas_call(
        _na_bias_kernel,
        grid_spec=pltpu.PrefetchScalarGridSpec(
            num_scalar_prefetch=1,
            grid=(h, n_v),
            in_specs=[pl.BlockSpec((1, n_dr + 1, GRID_W, GRID_W), lambda hh, v, ids_ref: (hh, 0, 0, 0))],
            out_specs=pl.BlockSpec((1, 1, NA_KROWS * GRID_W, NA_TQ), lambda hh, v, ids_ref: (hh, v, 0, 0))),
        out_shape=jax.ShapeDtypeStruct((h, n_v, NA_KROWS * GRID_W, NA_TQ), F32),
        compiler_params=_params(2),
        name="na_bias",
    )(jnp.asarray(ids.reshape(-1)), tiles_t)


def _na_bias_kernel(ids_ref, tiles_ref, o_ref):
    v = pl.program_id(1)
    for kr in range(NA_KROWS):
        for i in range(0, NA_QROWS, 2):
            base = (v * NA_KROWS + kr) * NA_QROWS + i
            pair = jnp.concatenate([tiles_ref[0, ids_ref[base]], tiles_ref[0, ids_ref[base + 1]]], axis=-1)
            o_ref[0, 0, pl.ds(GRID_W * kr, GRID_W), pl.ds(GRID_W * i, 2 * GRID_W)] = pair


def _na_attention(q, k, vt, k_ctx, vt_ctx, bias):
    b, n, _ = q.shape
    step_q = NA_SUBS * NA_TQ
    nj = n // step_q
    n_sub = n // NA_TQ
    n_kblk = n // NA_KBLK
    n_ctx = k_ctx.shape[1]
    rows_per_blk = NA_KBLK // GRID_W

    def kstart(j, sub):
        t = NA_SUBS * j + sub
        return jnp.clip((NA_QROWS * t - WIN_H // 2) // rows_per_blk, 0, n_kblk - NA_KPIECES)

    def kspec(sub, i):
        return pl.BlockSpec((1, NA_KBLK, LANES), lambda p, j, bb: (bb, kstart(j, sub) + i, p))

    def vspec(sub, i):
        return pl.BlockSpec((1, LANES, NA_KBLK), lambda p, j, bb: (bb, p, kstart(j, sub) + i))

    def bspec(sub):
        def variant(j):
            t = NA_SUBS * j + sub
            return jnp.where(t == 0, 0, jnp.where(t == n_sub - 1, 2, 1))
        return pl.BlockSpec((2, 1, NA_KROWS * GRID_W, NA_TQ), lambda p, j, bb: (p, variant(j), 0, 0))

    subs_pieces = [(sub, i) for sub in range(NA_SUBS) for i in range(NA_KPIECES)]
    return pl.pallas_call(
        _na_kernel,
        grid=(2, nj, b),
        in_specs=[pl.BlockSpec((1, step_q, LANES), lambda p, j, bb: (bb, j, p))]
        + [kspec(sub, i) for sub, i in subs_pieces] + [vspec(sub, i) for sub, i in subs_pieces]
        + [pl.BlockSpec((1, n_ctx, LANES), lambda p, j, bb: (bb, 0, p)),
           pl.BlockSpec((1, LANES, n_ctx), lambda p, j, bb: (bb, p, 0))]
        + [bspec(sub) for sub in range(NA_SUBS)],
        out_specs=pl.BlockSpec((1, step_q, LANES), lambda p, j, bb: (bb, j, p)),
        out_shape=jax.ShapeDtypeStruct((b, n, 256), BF16),
        compiler_params=_params(3),
        name="na_lat",
    )(q, *([k] * len(subs_pieces)), *([vt] * len(subs_pieces)), k_ctx, vt_ctx, *([bias] * NA_SUBS))


LRU_SEG = 8
LRU_PAD = 8
LRU_ROWS = 512
LRU_HALVES = 2


def _pitch(n):
    seg = n // LRU_SEG
    return seg + 8 if (seg // 8) % 2 == 0 else seg


def _pieces(n, t0, r):
    seg, out, t = n // LRU_SEG, [], t0
    while t < t0 + r:
        length = min(seg - t % seg, t0 + r - t)
        out.append((t - t0, (t // seg) * _pitch(n) + t % seg, length))
        t += length
    return out


def _put(ref, n, t0, val):
    for off, row, length in _pieces(n, t0, val.shape[0]):
        for hv in range(LRU_HALVES):
            ref[hv, pl.ds(row, length), :] = val[off:off + length, LANES * hv:LANES * (hv + 1)]


def _get(ref, n, t0, r):
    return jnp.concatenate(
        [jnp.concatenate([ref[hv, pl.ds(row, length), :] for hv in range(LRU_HALVES)], axis=-1)
         for _, row, length in _pieces(n, t0, r)], axis=0)


def _lru_gates(xp, n, cw, cb, wa_ref, ba_ref, wx_ref, bx_ref, sp, a_refs, u_refs):
    r = min(LRU_ROWS, n)
    for c0 in range(0, n, r):
        conv = cb
        for tap in range(CONV_W):
            conv = conv + cw[tap:tap + 1, :] * xp[pl.ds(LRU_PAD + c0 + tap - CONV_W // 2, r), :]
        xb = conv.astype(BF16)
        for d in range(2):
            rg = _sigmoid(_dot(xb, wa_ref[d]) + ba_ref[d:d + 1, :])
            ig = _sigmoid(_dot(xb, wx_ref[d]) + bx_ref[d:d + 1, :])
            a = jnp.exp((-LRU_C) * rg * sp[d:d + 1, :])
            _put(a_refs[d], n, c0, a)
            x1 = 1.0 - a * a
            root = jnp.where(x1 > 0.0, x1 * lax.rsqrt(x1), 0.0)
            _put(u_refs[d], n, c0, root * (ig * conv))


def _lru_scan(n, af, uf, ab, ub, hf0, hb0):
    seg, pitch = n // LRU_SEG, _pitch(n)

    def body(i, carry):
        rf = pl.ds(i, LRU_SEG, stride=pitch)
        rb = pl.ds(seg - 1 - i, LRU_SEG, stride=pitch)
        out = []
        for hv in range(LRU_HALVES):
            hlf, cmf, hlb, cmb = carry[4 * hv:4 * hv + 4]
            a1, u1 = af[hv, rf, :], uf[hv, rf, :]
            a2, u2 = ab[hv, rb, :], ub[hv, rb, :]
            hlf = a1 * hlf + u1
            cmf = a1 * cmf
            hlb = a2 * hlb + u2
            cmb = a2 * cmb
            uf[hv, rf, :] = hlf
            af[hv, rf, :] = cmf
            ub[hv, rb, :] = hlb
            ab[hv, rb, :] = cmb
            out += [hlf, cmf, hlb, cmb]
        return tuple(out)

    zero = jnp.zeros((LRU_SEG, LANES), F32)
    one = jnp.ones((LRU_SEG, LANES), F32)
    lax.fori_loop(0, seg, body, (zero, one, zero, one) * LRU_HALVES)
    hf, hb = list(hf0), list(hb0)
    for s in range(LRU_SEG):
        rf = pl.ds(s * pitch, seg)
        rb = pl.ds((LRU_SEG - 1 - s) * pitch, seg)
        for hv in range(LRU_HALVES):
            h = uf[hv, rf, :] + af[hv, rf, :] * hf[hv]
            uf[hv, rf, :] = h
            hf[hv] = h[seg - 1:seg, :]
            h = ub[hv, rb, :] + ab[hv, rb, :] * hb[hv]
            ub[hv, rb, :] = h
            hb[hv] = h[0:1, :]
    return hf, hb


def _lru_kernel(xl_ref, xc_ref, cw_ref, cb_ref, wa_ref, ba_ref, wx_ref, bx_ref, lam_ref,
                yl_ref, yc_ref, xp, afl, ufl, abl, ubl, afc, ufc, abc, ubc):
    n_lat, w = xl_ref.shape[1], xl_ref.shape[2]
    n_ctx = xc_ref.shape[1]
    cw, cb = cw_ref[...], cb_ref[...]
    neg_lam = -lam_ref[...]
    sp = jnp.maximum(neg_lam, 0.0) + jnp.log1p(jnp.exp(-jnp.abs(neg_lam)))
    zpad = jnp.zeros((LRU_PAD, w), F32)

    xp[pl.ds(0, LRU_PAD), :] = zpad
    xp[pl.ds(LRU_PAD, n_ctx), :] = xc_ref[0]
    xp[pl.ds(LRU_PAD + n_ctx, LRU_PAD), :] = zpad
    _lru_gates(xp, n_ctx, cw, cb, wa_ref, ba_ref, wx_ref, bx_ref, sp, (afc, abc), (ufc, ubc))
    xp[pl.ds(LRU_PAD, n_lat), :] = xl_ref[0]
    xp[pl.ds(LRU_PAD + n_lat, LRU_PAD), :] = zpad
    _lru_gates(xp, n_lat, cw, cb, wa_ref, ba_ref, wx_ref, bx_ref, sp, (afl, abl), (ufl, ubl))

    h0 = [jnp.zeros((1, LANES), F32)] * LRU_HALVES
    hf, hb = _lru_scan(n_ctx, afc, ufc, abc, ubc, h0, h0)
    _lru_scan(n_lat, afl, ufl, abl, ubl, hf, hb)
    yc_ref[0] = (_get(ufc, n_ctx, 0, n_ctx) + _get(ubc, n_ctx, 0, n_ctx)).astype(BF16)
    for c0 in range(0, n_lat, LRU_ROWS):
        yl_ref[0, pl.ds(c0, LRU_ROWS), :] = (_get(ufl, n_lat, c0, LRU_ROWS)
                                             + _get(ubl, n_lat, c0, LRU_ROWS)).astype(BF16)


def _lru(x_lat, x_ctx, conv_w, conv_b, wa, ba, wx, bx, lam):
    b, n_lat, w = x_lat.shape
    n_ctx = x_ctx.shape[1]
    consts = (conv_w, conv_b, wa, ba, wx, bx, lam)
    big = lambda: pltpu.VMEM((LRU_HALVES, LRU_SEG * _pitch(n_lat), LANES), F32)
    small = lambda: pltpu.VMEM((LRU_HALVES, LRU_SEG * _pitch(n_ctx), LANES), F32)
    return pl.pallas_call(
        _lru_kernel,
        grid=(b,),
        in_specs=[pl.BlockSpec((1, n_lat, w), lambda bb: (bb, 0, 0)),
                  pl.BlockSpec((1, n_ctx, w), lambda bb: (bb, 0, 0))]
        + [_const_spec(c.shape) for c in consts],
        out_specs=[pl.BlockSpec((1, n_lat, w), lambda bb: (bb, 0, 0)),
                   pl.BlockSpec((1, n_ctx, w), lambda bb: (bb, 0, 0))],
        out_shape=[jax.ShapeDtypeStruct((b, n_lat, w), BF16),
                   jax.ShapeDtypeStruct((b, n_ctx, w), BF16)],
        scratch_shapes=[pltpu.VMEM((n_lat + 2 * LRU_PAD, w), F32),
                        big(), big(), big(), big(), small(), small(), small(), small()],
        compiler_params=_params(1),
        name="rglru",
    )(x_lat, x_ctx, *consts)


MERGE_SUB_ROWS = 256


def _merge_kernel(x_ref, mod_ref, ya_ref, yb_ref, yc_ref, yd_ref, wz_ref, wm_ref, wb_ref, wo_ref,
                  lng_ref, lnb_ref, o_ref):
    shift = mod_ref[0, 0:1, :]
    scale = mod_ref[0, 1:2, :]
    gate = mod_ref[0, 2:3, :]
    tile = x_ref.shape[1]
    sub = min(tile, MERGE_SUB_ROWS)
    for r0 in range(0, tile, sub):
        rows = pl.ds(r0, sub)
        x = x_ref[0, rows, :]
        xm = (_layer_norm(x) * (1.0 + scale) + shift).astype(BF16)
        z = _dot(xm, wz_ref[...])
        acc = None
        for i, y_ref in enumerate((ya_ref, yb_ref, yc_ref, yd_ref)):
            zi = z[:, BRANCH_W * i:BRANCH_W * (i + 1)]
            g = (y_ref[0, rows, :].astype(F32) * (zi * _sigmoid(zi))).astype(BF16)
            t = _dot(g, wb_ref[i])
            mi = _dot(xm, wm_ref[:, D_MODEL * i:D_MODEL * (i + 1)])
            term = _sigmoid(mi) * t
            acc = term if acc is None else acc + term
        out = _dot(acc.astype(BF16), wo_ref[...])
        o_ref[0, rows, :] = _layer_norm(DEEPNORM_ALPHA * x + gate * out) * lng_ref[...] + lnb_ref[...]


def _merge(x, mod, ys, wz, wm, wb, wo, ln_g, ln_b, tile):
    b, n, _ = x.shape
    consts = (wz, wm, wb, wo, ln_g, ln_b)
    tok = lambda w: pl.BlockSpec((1, tile, w), lambda bb, t: (bb, t, 0))
    return pl.pallas_call(
        _merge_kernel,
        grid=(b, n // tile),
        in_specs=[tok(D_MODEL), pl.BlockSpec((1, 3, D_MODEL), lambda bb, t: (bb, 0, 0))]
        + [tok(BRANCH_W) for _ in ys] + [_const_spec(c.shape, single_buffer=True) for c in consts],
        out_specs=tok(D_MODEL),
        out_shape=jax.ShapeDtypeStruct((b, n, D_MODEL), F32),
        compiler_params=_params(2),
        name="merge",
    )(x, mod, *ys, *consts)


def _rope_tables(n):
    t = jnp.arange(n, dtype=jnp.int32)
    rows = (t // GRID_W).astype(F32)[:, None]
    cols = (t % GRID_W).astype(F32)[:, None]

    def table(half, offsets):
        inv = ROPE_THETA ** (-np.arange(0, 2 * half, 2, dtype=np.float64) / (2 * half))
        f_row, f_col, sign = (np.zeros(LANES, np.float32) for _ in range(3))
        for offset in offsets:
            for g, f in enumerate((f_row, f_col)):
                base = offset + 2 * half * g
                f[base:base + 2 * half] = np.concatenate([inv, inv])
                sign[base:base + 2 * half] = np.concatenate([-np.ones(half), np.ones(half)])
        ang = rows * f_row[None, :] + cols * f_col[None, :]
        return jnp.cos(ang), jnp.sin(ang) * sign[None, :]

    cosd, sind = table(16, (0, HEAD_DIM))
    cosb, sinb = table(8, (MLA_NOPE,))
    cosk, sink = table(8, (0,))
    return cosd, sind, cosb, sinb, cosk, sink


def _identity_tables(n):
    one, zero = jnp.ones((n, LANES), F32), jnp.zeros((n, LANES), F32)
    return one, zero, one, zero, one, zero


def _layer_weights(w_in, mla_q_norm, mla_w_uq, mla_kv_norm, mla_w_ukv, gqa_q_norm, gqa_k_norm, w_branch, w_out):
    w_in = w_in.astype(BF16)
    wi = w_in[:, :MIX_COLS]
    qa, ka, va = wi[:, 0:256], wi[:, 256:512], wi[:, 512:768]
    cq, ckv, kr = wi[:, 768:1024], wi[:, 1024:1152], wi[:, 1152:1184]
    lru, qd, kd, vd = wi[:, 1184:1440], wi[:, 1440:1696], wi[:, 1696:1824], wi[:, 1824:1952]
    head_order = jnp.array([0, 2, 1, 3])
    qd = qd.reshape(D_MODEL, 4, HEAD_DIM)[:, head_order].reshape(D_MODEL, 256)
    wmix = jnp.concatenate([qa, ka, va, cq, ckv, lru, qd, kd, vd, kr,
                            jnp.zeros((D_MODEL, MIX_PAD - MIX_COLS), BF16)], -1)
    hq = MLA_NOPE + MLA_ROPE
    wuq = jnp.pad(mla_w_uq.reshape(256, MLA_HEADS, hq), ((0, 0), (0, 0), (0, LANES - hq)))
    wuq = wuq.reshape(256, MLA_HEADS * LANES).astype(BF16)
    ukv = mla_w_ukv.reshape(128, MLA_HEADS, 128)
    wuk = jnp.pad(ukv[:, :, :MLA_NOPE], ((0, 0), (0, 0), (0, LANES - MLA_NOPE)))
    wuk = wuk.reshape(128, MLA_HEADS * LANES).astype(BF16)
    wuv = ukv[:, :, MLA_NOPE:].reshape(128, 256).astype(BF16)
    pk = np.zeros((LANES, MLA_HEADS * LANES), np.float32)
    for h in range(MLA_HEADS):
        pk[np.arange(MLA_ROPE), LANES * h + MLA_NOPE + np.arange(MLA_ROPE)] = 1.0
    grp = np.kron(np.eye(4, dtype=np.float32), np.ones((HEAD_DIM, HEAD_DIM), np.float32))
    inproj_w = (wmix, wuq, wuk, wuv, jnp.asarray(pk, BF16), jnp.asarray(grp, BF16),
                mla_q_norm.reshape(1, 256), mla_kv_norm.reshape(1, 128),
                jnp.tile(gqa_q_norm, 4).reshape(1, 256), jnp.tile(gqa_k_norm, 2).reshape(1, 128))
    wz = w_in[:, MIX_COLS:MIX_COLS + SILU_COLS]
    wz_d = wz[:, 3 * BRANCH_W:].reshape(D_MODEL, 4, HEAD_DIM)[:, head_order].reshape(D_MODEL, BRANCH_W)
    wz = jnp.concatenate([wz[:, :3 * BRANCH_W], wz_d], -1)
    wm = w_in[:, MIX_COLS + SILU_COLS:]
    wb_d = w_branch[3].reshape(4, HEAD_DIM, D_MODEL)[head_order].reshape(BRANCH_W, D_MODEL)
    wb = jnp.concatenate([w_branch[:3], wb_d[None]], 0).astype(BF16)
    return inproj_w, (wz, wm, wb, w_out.astype(BF16))


def _block_diag(w):
    eye = jnp.eye(4, dtype=w.dtype)
    return jnp.einsum("dkce,kj->dkcje", w, eye).reshape(2, 256, 256)


def kernel(x, c, ctx, c_ctx, w_mod, b_mod, w_in, na_rel_bias, mla_q_norm, mla_w_uq, mla_kv_norm, mla_w_ukv,
           lru_conv_w, lru_conv_b, lru_w_a, lru_b_a, lru_w_x, lru_b_x, lru_lambda, gqa_q_norm, gqa_k_norm,
           w_branch, w_out, ln_g, ln_b):
    b, n, _ = x.shape
    n_ctx = ctx.shape[1]
    rows = 8 * ((b + 1 + 7) // 8)
    c_all = jnp.zeros((rows, D_MODEL), F32).at[:b].set(c).at[b].set(c_ctx)
    mod_all = _modulation(c_all, w_mod, b_mod)
    lat_tables = _rope_tables(n)
    ctx_tables = _identity_tables(n_ctx)

    for l in range(DEPTH):
        need_ctx = l < DEPTH - 1
        mod_lat = mod_all[l, :b].reshape(b, 3, D_MODEL)
        mod_ctx = jnp.broadcast_to(mod_all[l, b].reshape(1, 3, D_MODEL), (b, 3, D_MODEL))
        inproj_w, merge_w = _layer_weights(w_in[l], mla_q_norm[l], mla_w_uq[l], mla_kv_norm[l], mla_w_ukv[l],
                                           gqa_q_norm[l], gqa_k_norm[l], w_branch[l], w_out[l])
        qa, ka, va, qb, kb, vb, xr, qd, kd, vd, vat, vbt, vdt = _inproj(x, mod_lat, inproj_w, lat_tables, 1024)
        qa_c, ka_c, va_c, qb_c, kb_c, vb_c, xr_c, qd_c, kd_c, vd_c, vat_c, vbt_c, vdt_c = _inproj(
            ctx, mod_ctx, inproj_w, ctx_tables, n_ctx)

        ya = _na_attention(qa, ka, vat, ka_c, vat_c, _na_bias(na_rel_bias[l]))
        yb = _attention_t(qb, kb, vbt, kb_c, vbt_c, mode="B", tq=2048, ahead=4)
        yd = _attention_t(qd, kd, vdt, kd_c, vdt_c, mode="D", tq=2048, ahead=5)
        yc, yc_c = _lru(xr, xr_c, lru_conv_w[l], lru_conv_b[l].reshape(1, -1),
                        _block_diag(lru_w_a[l]).astype(BF16), lru_b_a[l],
                        _block_diag(lru_w_x[l]).astype(BF16), lru_b_x[l], lru_lambda[l])
        tail = (*merge_w, ln_g[l].reshape(1, -1), ln_b[l].reshape(1, -1))
        x_new = _merge(x, mod_lat, (ya, yb, yc, yd), *tail, 1024)
        if need_ctx:
            ya_c = _attention(qa_c, None, None, ka_c, va_c, mode="A", tq=n_ctx)
            yb_c = _attention(qb_c, None, None, kb_c, vb_c, mode="B", tq=n_ctx)
            yd_c = _attention(qd_c, None, None, kd_c, vd_c, mode="D", tq=n_ctx)
            ctx = _merge(ctx, mod_ctx, (ya_c, yb_c, yc_c, yd_c), *tail, n_ctx)
        x = x_new
    return x
```

```python
import functools

import numpy as np
import jax
import jax.numpy as jnp
from jax import lax
from jax.experimental import pallas as pl
from jax.experimental.pallas import tpu as pltpu

F32 = jnp.float32
BF16 = jnp.bfloat16

D_MODEL = 1024
DEPTH = 2
GRID_W = 64
BRANCH_W = 256
HEAD_DIM = 64
WIN_H = 8
WIN_W = 16
MLA_HEADS = 4
MLA_NOPE = 64
MLA_ROPE = 32
LRU_C = 8.0
CONV_W = 4
ROPE_THETA = 10000.0
EPS = 1e-6
LOG2E = 1.4426950408889634
NA_SCALE = HEAD_DIM ** -0.5
MLA_SCALE = (MLA_NOPE + MLA_ROPE) ** -0.5
GQA_SCALE = HEAD_DIM ** -0.5
DEEPNORM_ALPHA = (2 * DEPTH) ** 0.25

MIX_COLS = 1952
SILU_COLS = 1024
LANES = 128
NEG_BIG = -1e30
VMEM_LIMIT = 56 * 1024 * 1024

OFF_QA, OFF_KA, OFF_VA, OFF_CQ, OFF_CKV, OFF_LRU, OFF_QD, OFF_KD, OFF_VD, OFF_KR = (
    0, 256, 512, 768, 1024, 1152, 1408, 1664, 1792, 1920)
MIX_PAD = 2048


def _dot(a, b):
    return jnp.dot(a, b, preferred_element_type=F32)


def _dot_t(a, b):
    return lax.dot_general(a, b, (((1,), (1,)), ((), ())), preferred_element_type=F32)


def _layer_norm(x):
    mu = jnp.mean(x, axis=-1, keepdims=True)
    xc = x - mu
    var = jnp.mean(xc * xc, axis=-1, keepdims=True)
    return xc * lax.rsqrt(var + EPS)


def _sigmoid(x):
    return 0.5 * jnp.tanh(0.5 * x) + 0.5


def _rms(x, g):
    return x * lax.rsqrt(jnp.mean(x * x, axis=-1, keepdims=True) + EPS) * g


def _params(n_grid):
    return pltpu.CompilerParams(dimension_semantics=("arbitrary",) * n_grid,
                                vmem_limit_bytes=VMEM_LIMIT)


def _const_spec(shape, single_buffer=False):
    zeros = (0,) * len(shape)
    mode = pl.Buffered(1) if single_buffer else None
    return pl.BlockSpec(shape, lambda *_: zeros, pipeline_mode=mode)


def _mod_kernel(c_ref, w_ref, b_ref, o_ref):
    c = c_ref[...]
    s = (c * jax.nn.sigmoid(c)).astype(BF16)
    o_ref[0] = _dot(s, w_ref[0].astype(BF16)) + b_ref[0]


def _modulation(c_all, w_mod, b_mod):
    n_l, _, n_out = w_mod.shape
    rows = c_all.shape[0]
    bn = 512
    return pl.pallas_call(
        _mod_kernel,
        grid=(n_l, n_out // bn),
        in_specs=[pl.BlockSpec((rows, D_MODEL), lambda l, n: (0, 0)),
                  pl.BlockSpec((1, D_MODEL, bn), lambda l, n: (l, 0, n)),
                  pl.BlockSpec((1, 1, bn), lambda l, n: (l, 0, n))],
        out_specs=pl.BlockSpec((1, rows, bn), lambda l, n: (l, 0, n)),
        out_shape=jax.ShapeDtypeStruct((n_l, rows, n_out), F32),
        compiler_params=_params(2),
        name="modulation",
    )(c_all, w_mod, b_mod.reshape(n_l, 1, n_out))


def _pair_select(lane, shift):
    lane_up = pltpu.roll(lane, LANES - shift, 1)
    want = jnp.where((lane & (2 * shift - 1)) < shift, lane + shift, lane - shift)
    return lane_up == want


def _rope(x, cos, sin, shift, sel):
    partner = jnp.where(sel, pltpu.roll(x, LANES - shift, 1), pltpu.roll(x, shift, 1))
    return x * cos + partner * sin


def _group_sumsq(x, g):
    sq = x * x
    hi = sq.astype(BF16)
    lo = (sq - hi.astype(F32)).astype(BF16)
    return _dot(hi, g) + _dot(lo, g)


INPROJ_SUB_ROWS = 512


def _inproj_kernel(x_ref, mod_ref, wmix_ref, wuq_ref, wuk_ref, wuv_ref, pk_ref, g_ref,
                   gq_ref, gkv_ref, gdq_ref, gdk_ref,
                   cosd_ref, sind_ref, cosb_ref, sinb_ref, cosk_ref, sink_ref,
                   qa_o, ka_o, va_o, qb_o, kb_o, vb_o, xc_o, qd_o, kd_o, vd_o, vat_o, vbt_o, vdt_o):
    shift = mod_ref[0, 0:1, :]
    scale = mod_ref[0, 1:2, :]
    tile = x_ref.shape[1]
    sub = min(tile, INPROJ_SUB_ROWS)
    lane = lax.broadcasted_iota(jnp.int32, (sub, LANES), 1)
    sel16 = _pair_select(lane, 16)
    sel8 = _pair_select(lane, 8)
    for r0 in range(0, tile, sub):
        rows = pl.ds(r0, sub)
        xm = (_layer_norm(x_ref[0, rows, :]) * (1.0 + scale) + shift).astype(BF16)
        p = _dot(xm, wmix_ref[...])

        qa_o[0, rows, :] = (p[:, OFF_QA:OFF_QA + 256] * (NA_SCALE * LOG2E)).astype(BF16)
        ka_o[0, rows, :] = p[:, OFF_KA:OFF_KA + 256].astype(BF16)
        va = p[:, OFF_VA:OFF_VA + 256]
        va_o[0, rows, :] = va.astype(BF16)
        vat_o[0, :, rows] = va.T.astype(BF16)

        cqn = _rms(p[:, OFF_CQ:OFF_CQ + 256], gq_ref[...]).astype(BF16)
        qb = _dot(cqn, wuq_ref[...])
        cosb, sinb = cosb_ref[rows, :], sinb_ref[rows, :]
        for h in range(MLA_HEADS):
            blk = _rope(qb[:, LANES * h:LANES * (h + 1)], cosb, sinb, 8, sel8)
            qb_o[0, rows, LANES * h:LANES * (h + 1)] = (blk * (MLA_SCALE * LOG2E)).astype(BF16)
        ckvn = _rms(p[:, OFF_CKV:OFF_CKV + 128], gkv_ref[...]).astype(BF16)
        kr = _rope(p[:, OFF_KR:OFF_KR + 128], cosk_ref[rows, :], sink_ref[rows, :], 8, sel8).astype(BF16)
        kb_o[0, rows, :] = (_dot(ckvn, wuk_ref[...]) + _dot(kr, pk_ref[...])).astype(BF16)
        vb = _dot(ckvn, wuv_ref[...])
        vb_o[0, rows, :] = vb.astype(BF16)
        vbt_o[0, :, rows] = vb.T.astype(BF16)

        xc_o[0, rows, :] = p[:, OFF_LRU:OFF_LRU + 256]

        cosd, sind = cosd_ref[rows, :], sind_ref[rows, :]
        qd = p[:, OFF_QD:OFF_QD + 256]
        qd = qd * lax.rsqrt(_group_sumsq(qd, g_ref[...]) * (1.0 / HEAD_DIM) + EPS) * gdq_ref[...]
        for blk_i in range(2):
            blk = _rope(qd[:, LANES * blk_i:LANES * (blk_i + 1)], cosd, sind, 16, sel16)
            qd_o[0, rows, LANES * blk_i:LANES * (blk_i + 1)] = (blk * (GQA_SCALE * LOG2E)).astype(BF16)
        kd = p[:, OFF_KD:OFF_KD + 128]
        kd = kd * lax.rsqrt(_group_sumsq(kd, g_ref[0:128, 0:128]) * (1.0 / HEAD_DIM) + EPS) * gdk_ref[...]
        kd_o[0, rows, :] = _rope(kd, cosd, sind, 16, sel16).astype(BF16)
        vd = p[:, OFF_VD:OFF_VD + 128]
        vd_o[0, rows, :] = vd.astype(BF16)
        vdt_o[0, :, rows] = vd.T.astype(BF16)


def _inproj(x, mod, weights, tables, tile):
    b, n, _ = x.shape
    nt = n // tile
    wspecs = [_const_spec(w.shape) for w in weights]
    tspecs = [pl.BlockSpec((tile, LANES), lambda t, bb: (t, 0)) for _ in tables]
    widths = (256, 256, 256, 512, 512, 256, 256, 256, 128, 128)
    dtypes = (BF16, BF16, BF16, BF16, BF16, BF16, F32, BF16, BF16, BF16)
    t_rows = (256, 256, 128)
    return pl.pallas_call(
        _inproj_kernel,
        grid=(nt, b),
        in_specs=[pl.BlockSpec((1, tile, D_MODEL), lambda t, bb: (bb, t, 0)),
                  pl.BlockSpec((1, 3, D_MODEL), lambda t, bb: (bb, 0, 0))] + wspecs + tspecs,
        out_specs=[pl.BlockSpec((1, tile, w), lambda t, bb: (bb, t, 0)) for w in widths]
        + [pl.BlockSpec((1, r, tile), lambda t, bb: (bb, 0, t)) for r in t_rows],
        out_shape=[jax.ShapeDtypeStruct((b, n, w), dt) for w, dt in zip(widths, dtypes)]
        + [jax.ShapeDtypeStruct((b, r, n), BF16) for r in t_rows],
        compiler_params=_params(2),
        name="inproj",
    )(x, mod, *weights, *tables)


def _half_mask(lane, hh):
    return lane >= HEAD_DIM if hh else lane < HEAD_DIM


def _attn_ctx_kernel(q_ref, kc_ref, vc_ref, o_ref, *, masked):
    tq = q_ref.shape[1]
    lane = lax.broadcasted_iota(jnp.int32, (tq, LANES), 1)
    outs = []
    for hh in range(2):
        if masked:
            qh = jnp.where(_half_mask(lane, hh), q_ref[0], jnp.zeros((), BF16))
            ksl = slice(None)
        else:
            qh = q_ref[0, :, LANES * hh:LANES * (hh + 1)]
            ksl = slice(LANES * hh, LANES * (hh + 1))
        s = _dot_t(qh, kc_ref[0, :, ksl])
        pr = jnp.exp2(s - jnp.max(s, axis=-1, keepdims=True))
        acc = _dot(pr.astype(BF16), vc_ref[0])
        outs.append(acc * (1.0 / jnp.sum(pr, axis=-1, keepdims=True)))
    o_ref[0] = jnp.where(lane < HEAD_DIM, outs[0], outs[1]).astype(BF16)


def _attention_ctx(q, k_ctx, v_ctx, *, mode):
    b, nq, _ = q.shape
    n_ctx = k_ctx.shape[1]
    qw = 256 if mode == "B" else LANES
    kidx = (lambda bb, p: (bb, 0, 0)) if mode == "D" else (lambda bb, p: (bb, 0, p))
    return pl.pallas_call(
        functools.partial(_attn_ctx_kernel, masked=mode != "B"),
        grid=(b, 2),
        in_specs=[pl.BlockSpec((1, nq, qw), lambda bb, p: (bb, 0, p)),
                  pl.BlockSpec((1, n_ctx, qw), kidx), pl.BlockSpec((1, n_ctx, LANES), kidx)],
        out_specs=pl.BlockSpec((1, nq, LANES), lambda bb, p: (bb, 0, p)),
        out_shape=jax.ShapeDtypeStruct((b, nq, 256), BF16),
        compiler_params=_params(2),
        name="attn_" + mode + "_ctx",
    )(q, k_ctx, v_ctx)


def _with_ones_row(vt, hh):
    row = lax.broadcasted_iota(jnp.int32, vt.shape, 0)
    return jnp.where(row == _ones_row(hh), jnp.ones((), BF16), vt)


def _ones_row(hh):
    return 0 if hh else HEAD_DIM


def _softmax_update_t(s, vtc, m, acc):
    m_new = jnp.maximum(m, jnp.max(s, axis=0, keepdims=True))
    alpha = jnp.exp2(m - m_new)
    pr = jnp.exp2(s - m_new)
    acc = alpha * acc + _dot(vtc, pr.astype(BF16))
    return m_new, acc


def _finish_t(accs):
    outs = [acc * (1.0 / acc[_ones_row(hh):_ones_row(hh) + 1, :]) for hh, acc in enumerate(accs)]
    row = lax.broadcasted_iota(jnp.int32, outs[0].shape, 0)
    return jnp.where(row < HEAD_DIM, outs[0], outs[1]).T.astype(BF16)


def _attn_t_kernel(q_ref, kl_ref, vtl_ref, kc_ref, vtc_ref, o_ref, *, masked, chunk, ahead):
    tq = q_ref.shape[1]
    n_lat = kl_ref.shape[1]
    lane = lax.broadcasted_iota(jnp.int32, (tq, LANES), 1)
    qs, ksls = [], []
    for hh in range(2):
        if masked:
            qs.append(jnp.where(_half_mask(lane, hh), q_ref[0], jnp.zeros((), BF16)))
            ksls.append(slice(None))
        else:
            qs.append(q_ref[0, :, LANES * hh:LANES * (hh + 1)])
            ksls.append(slice(LANES * hh, LANES * (hh + 1)))
    carries = [(jnp.full((1, tq), NEG_BIG, F32), jnp.zeros((LANES, tq), F32)) for _ in range(2)]
    tasks = [(hh, (kl_ref, pl.ds(c0, chunk)), (vtl_ref, pl.ds(c0, chunk)))
             for c0 in range(0, n_lat, chunk) for hh in range(2)]
    tasks += [(hh, (kc_ref, slice(None)), (vtc_ref, slice(None))) for hh in range(2)]

    def scores(task):
        hh, (k_ref, rows), _ = task
        return _dot_t(k_ref[0, rows, ksls[hh]], qs[hh])

    pending = [scores(task) for task in tasks[:ahead]]
    for i, task in enumerate(tasks):
        s_cur = pending.pop(0)
        if i + ahead < len(tasks):
            pending.append(scores(tasks[i + ahead]))
        hh, _, (vt_ref, cols) = task
        carries[hh] = _softmax_update_t(s_cur, _with_ones_row(vt_ref[0, :, cols], hh), *carries[hh])
    o_ref[0] = _finish_t([acc for _, acc in carries])


def _attention_t(q, k_lat, vt_lat, k_ctx, vt_ctx, *, mode, tq, ahead, chunk=512):
    b, nq, _ = q.shape
    masked = mode != "B"
    qw = 256 if mode == "B" else LANES
    kidx = (lambda bb, p, t: (bb, 0, 0)) if mode == "D" else (lambda bb, p, t: (bb, 0, p))
    vidx = (lambda bb, p, t: (bb, 0, 0)) if mode == "D" else (lambda bb, p, t: (bb, p, 0))
    n_lat, n_ctx = k_lat.shape[1], k_ctx.shape[1]
    return pl.pallas_call(
        functools.partial(_attn_t_kernel, masked=masked, chunk=chunk, ahead=ahead),
        grid=(b, 2, nq // tq),
        in_specs=[pl.BlockSpec((1, tq, qw), lambda bb, p, t: (bb, t, p)),
                  pl.BlockSpec((1, n_lat, qw), kidx), pl.BlockSpec((1, LANES, n_lat), vidx),
                  pl.BlockSpec((1, n_ctx, qw), kidx), pl.BlockSpec((1, LANES, n_ctx), vidx)],
        out_specs=pl.BlockSpec((1, tq, LANES), lambda bb, p, t: (bb, t, p)),
        out_shape=jax.ShapeDtypeStruct((b, nq, 256), BF16),
        compiler_params=_params(3),
        name="attn_" + mode + "_lat",
    )(q, k_lat, vt_lat, k_ctx, vt_ctx)


NA_QROWS = 4
NA_KROWS = 12
NA_TQ = NA_QROWS * GRID_W
NA_KBLK = 256
NA_KPIECES = NA_KROWS * GRID_W // NA_KBLK
NA_SUBS = 2


def _na_kernel(q_ref, *refs):
    k_refs, refs = refs[:NA_SUBS * NA_KPIECES], refs[NA_SUBS * NA_KPIECES:]
    vt_refs, refs = refs[:NA_SUBS * NA_KPIECES], refs[NA_SUBS * NA_KPIECES:]
    kc_ref, vtc_ref = refs[:2]
    bias_refs, o_ref = refs[2:2 + NA_SUBS], refs[2 + NA_SUBS]
    lane = lax.broadcasted_iota(jnp.int32, (NA_TQ, LANES), 1)
    work = []
    for sub in range(NA_SUBS):
        q = q_ref[0, pl.ds(sub * NA_TQ, NA_TQ), :]
        pieces = slice(sub * NA_KPIECES, (sub + 1) * NA_KPIECES)
        k_win = jnp.concatenate([r[0] for r in k_refs[pieces]], axis=0)
        vt_win = jnp.concatenate([r[0] for r in vt_refs[pieces]], axis=1)
        for hh in range(2):
            qh = jnp.where(_half_mask(lane, hh), q, jnp.zeros((), BF16))
            work.append((vt_win, hh, _dot_t(k_win, qh) + bias_refs[sub][hh, 0], _dot_t(kc_ref[0], qh)))
    accs = []
    for vt_win, hh, s_win, s_ctx in work:
        m = jnp.maximum(jnp.max(s_win, axis=0, keepdims=True), jnp.max(s_ctx, axis=0, keepdims=True))
        p_win = jnp.exp2(s_win - m).astype(BF16)
        p_ctx = jnp.exp2(s_ctx - m).astype(BF16)
        accs.append(_dot(_with_ones_row(vt_win, hh), p_win) + _dot(_with_ones_row(vtc_ref[0], hh), p_ctx))
    for sub in range(NA_SUBS):
        o_ref[0, pl.ds(sub * NA_TQ, NA_TQ), :] = _finish_t(accs[2 * sub:2 * sub + 2])


def _na_bias(rel_bias):
    h, n_dr, n_dc = rel_bias.shape
    n_rows = 64
    w2 = 2 * GRID_W
    pad_lo = GRID_W - WIN_W
    p = jnp.pad(rel_bias * LOG2E, ((0, 0), (0, 0), (pad_lo, w2 - pad_lo - n_dc)))
    skew = jnp.broadcast_to(p[:, :, None, :], (h, n_dr, GRID_W, w2)).reshape(h, n_dr, GRID_W * w2)
    skew = skew[:, :, :GRID_W * (w2 - 1)].reshape(h, n_dr, GRID_W, w2 - 1)
    tiles = skew[..., GRID_W - 1:]
    qc = np.arange(GRID_W)
    cs = np.clip(qc - WIN_W // 2, 0, GRID_W - WIN_W)
    in_col = (qc[None, :] >= cs[:, None]) & (qc[None, :] < cs[:, None] + WIN_W)
    tiles = jnp.where(in_col, tiles, NEG_BIG)
    tiles = jnp.concatenate([tiles, jnp.full((h, 1, GRID_W, GRID_W), NEG_BIG, F32)], axis=1)
    ids = np.full((3, NA_KROWS, NA_QROWS), n_dr, np.int32)
    for v, t in enumerate((0, 1, n_rows // NA_QROWS - 1)):
        kb = int(np.clip(NA_QROWS * t - WIN_H // 2, 0, n_rows - NA_KROWS))
        for i in range(NA_QROWS):
            qr = NA_QROWS * t + i
            rs = int(np.clip(qr - WIN_H // 2, 0, n_rows - WIN_H))
            for kr in range(rs, rs + WIN_H):
                ids[v, kr - kb, i] = kr - qr + WIN_H - 1
    tiles_t = jnp.swapaxes(tiles, -1, -2)
    n_v = ids.shape[0]
    return pl.pallas_call(
        _na_bias_kernel,
        grid_spec=pltpu.PrefetchScalarGridSpec(
            num_scalar_prefetch=1,
            grid=(h, n_v),
            in_specs=[pl.BlockSpec((1, n_dr + 1, GRID_W, GRID_W), lambda hh, v, ids_ref: (hh, 0, 0, 0))],
            out_specs=pl.BlockSpec((1, 1, NA_KROWS * GRID_W, NA_TQ), lambda hh, v, ids_ref: (hh, v, 0, 0))),
        out_shape=jax.ShapeDtypeStruct((h, n_v, NA_KROWS * GRID_W, NA_TQ), F32),
        compiler_params=_params(2),
        name="na_bias",
    )(jnp.asarray(ids.reshape(-1)), tiles_t)


def _na_bias_kernel(ids_ref, tiles_ref, o_ref):
    v = pl.program_id(1)
    for kr in range(NA_KROWS):
        for i in range(0, NA_QROWS, 2):
            base = (v * NA_KROWS + kr) * NA_QROWS + i
            pair = jnp.concatenate([tiles_ref[0, ids_ref[base]], tiles_ref[0, ids_ref[base + 1]]], axis=-1)
            o_ref[0, 0, pl.ds(GRID_W * kr, GRID_W), pl.ds(GRID_W * i, 2 * GRID_W)] = pair


def _na_attention(q, k, vt, k_ctx, vt_ctx, bias):
    b, n, _ = q.shape
    step_q = NA_SUBS * NA_TQ
    nj = n // step_q
    n_sub = n // NA_TQ
    n_kblk = n // NA_KBLK
    n_ctx = k_ctx.shape[1]
    rows_per_blk = NA_KBLK // GRID_W

    def kstart(j, sub):
        t = NA_SUBS * j + sub
        return jnp.clip((NA_QROWS * t - WIN_H // 2) // rows_per_blk, 0, n_kblk - NA_KPIECES)

    def kspec(sub, i):
        return pl.BlockSpec((1, NA_KBLK, LANES), lambda p, j, bb: (bb, kstart(j, sub) + i, p))

    def vspec(sub, i):
        return pl.BlockSpec((1, LANES, NA_KBLK), lambda p, j, bb: (bb, p, kstart(j, sub) + i))

    def bspec(sub):
        def variant(j):
            t = NA_SUBS * j + sub
            return jnp.where(t == 0, 0, jnp.where(t == n_sub - 1, 2, 1))
        return pl.BlockSpec((2, 1, NA_KROWS * GRID_W, NA_TQ), lambda p, j, bb: (p, variant(j), 0, 0))

    subs_pieces = [(sub, i) for sub in range(NA_SUBS) for i in range(NA_KPIECES)]
    return pl.pallas_call(
        _na_kernel,
        grid=(2, nj, b),
        in_specs=[pl.BlockSpec((1, step_q, LANES), lambda p, j, bb: (bb, j, p))]
        + [kspec(sub, i) for sub, i in subs_pieces] + [vspec(sub, i) for sub, i in subs_pieces]
        + [pl.BlockSpec((1, n_ctx, LANES), lambda p, j, bb: (bb, 0, p)),
           pl.BlockSpec((1, LANES, n_ctx), lambda p, j, bb: (bb, p, 0))]
        + [bspec(sub) for sub in range(NA_SUBS)],
        out_specs=pl.BlockSpec((1, step_q, LANES), lambda p, j, bb: (bb, j, p)),
        out_shape=jax.ShapeDtypeStruct((b, n, 256), BF16),
        compiler_params=_params(3),
        name="na_lat",
    )(q, *([k] * len(subs_pieces)), *([vt] * len(subs_pieces)), k_ctx, vt_ctx, *([bias] * NA_SUBS))


LRU_SEG = 8
LRU_PAD = 8
LRU_ROWS = 512
LRU_HALVES = 2


def _pitch(n):
    seg = n // LRU_SEG
    return seg + 8 if (seg // 8) % 2 == 0 else seg


def _pieces(n, t0, r):
    seg, out, t = n // LRU_SEG, [], t0
    while t < t0 + r:
        length = min(seg - t % seg, t0 + r - t)
        out.append((t - t0, (t // seg) * _pitch(n) + t % seg, length))
        t += length
    return out


def _put(ref, n, t0, val):
    for off, row, length in _pieces(n, t0, val.shape[0]):
        for hv in range(LRU_HALVES):
            ref[hv, pl.ds(row, length), :] = val[off:off + length, LANES * hv:LANES * (hv + 1)]


def _get(ref, n, t0, r):
    return jnp.concatenate(
        [jnp.concatenate([ref[hv, pl.ds(row, length), :] for hv in range(LRU_HALVES)], axis=-1)
         for _, row, length in _pieces(n, t0, r)], axis=0)


def _lru_gates(xp, n, cw, cb, wa_ref, ba_ref, wx_ref, bx_ref, sp, a_refs, u_refs):
    r = min(LRU_ROWS, n)
    for c0 in range(0, n, r):
        conv = cb
        for tap in range(CONV_W):
            conv = conv + cw[tap:tap + 1, :] * xp[pl.ds(LRU_PAD + c0 + tap - CONV_W // 2, r), :]
        xb = conv.astype(BF16)
        for d in range(2):
            rg = _sigmoid(_dot(xb, wa_ref[d]) + ba_ref[d:d + 1, :])
            ig = _sigmoid(_dot(xb, wx_ref[d]) + bx_ref[d:d + 1, :])
            a = jnp.exp((-LRU_C) * rg * sp[d:d + 1, :])
            _put(a_refs[d], n, c0, a)
            x1 = 1.0 - a * a
            root = jnp.where(x1 > 0.0, x1 * lax.rsqrt(x1), 0.0)
            _put(u_refs[d], n, c0, root * (ig * conv))


def _lru_scan(n, af, uf, ab, ub, hf0, hb0):
    seg, pitch = n // LRU_SEG, _pitch(n)

    def body(i, carry):
        rf = pl.ds(i, LRU_SEG, stride=pitch)
        rb = pl.ds(seg - 1 - i, LRU_SEG, stride=pitch)
        out = []
        for hv in range(LRU_HALVES):
            hlf, cmf, hlb, cmb = carry[4 * hv:4 * hv + 4]
            a1, u1 = af[hv, rf, :], uf[hv, rf, :]
            a2, u2 = ab[hv, rb, :], ub[hv, rb, :]
            hlf = a1 * hlf + u1
            cmf = a1 * cmf
            hlb = a2 * hlb + u2
            cmb = a2 * cmb
            uf[hv, rf, :] = hlf
            af[hv, rf, :] = cmf
            ub[hv, rb, :] = hlb
            ab[hv, rb, :] = cmb
            out += [hlf, cmf, hlb, cmb]
        return tuple(out)

    zero = jnp.zeros((LRU_SEG, LANES), F32)
    one = jnp.ones((LRU_SEG, LANES), F32)
    lax.fori_loop(0, seg, body, (zero, one, zero, one) * LRU_HALVES)
    hf, hb = list(hf0), list(hb0)
    for s in range(LRU_SEG):
        rf = pl.ds(s * pitch, seg)
        rb = pl.ds((LRU_SEG - 1 - s) * pitch, seg)
        for hv in range(LRU_HALVES):
            h = uf[hv, rf, :] + af[hv, rf, :] * hf[hv]
            uf[hv, rf, :] = h
            hf[hv] = h[seg - 1:seg, :]
            h = ub[hv, rb, :] + ab[hv, rb, :] * hb[hv]
            ub[hv, rb, :] = h
            hb[hv] = h[0:1, :]
    return hf, hb


def _lru_kernel(xl_ref, xc_ref, cw_ref, cb_ref, wa_ref, ba_ref, wx_ref, bx_ref, lam_ref,
                yl_ref, yc_ref, xp, afl, ufl, abl, ubl, afc, ufc, abc, ubc):
    n_lat, w = xl_ref.shape[1], xl_ref.shape[2]
    n_ctx = xc_ref.shape[1]
    cw, cb = cw_ref[...], cb_ref[...]
    neg_lam = -lam_ref[...]
    sp = jnp.maximum(neg_lam, 0.0) + jnp.log1p(jnp.exp(-jnp.abs(neg_lam)))
    zpad = jnp.zeros((LRU_PAD, w), F32)

    xp[pl.ds(0, LRU_PAD), :] = zpad
    xp[pl.ds(LRU_PAD, n_ctx), :] = xc_ref[0]
    xp[pl.ds(LRU_PAD + n_ctx, LRU_PAD), :] = zpad
    _lru_gates(xp, n_ctx, cw, cb, wa_ref, ba_ref, wx_ref, bx_ref, sp, (afc, abc), (ufc, ubc))
    xp[pl.ds(LRU_PAD, n_lat), :] = xl_ref[0]
    xp[pl.ds(LRU_PAD + n_lat, LRU_PAD), :] = zpad
    _lru_gates(xp, n_lat, cw, cb, wa_ref, ba_ref, wx_ref, bx_ref, sp, (afl, abl), (ufl, ubl))

    h0 = [jnp.zeros((1, LANES), F32)] * LRU_HALVES
    hf, hb = _lru_scan(n_ctx, afc, ufc, abc, ubc, h0, h0)
    _lru_scan(n_lat, afl, ufl, abl, ubl, hf, hb)
    yc_ref[0] = (_get(ufc, n_ctx, 0, n_ctx) + _get(ubc, n_ctx, 0, n_ctx)).astype(BF16)
    for c0 in range(0, n_lat, LRU_ROWS):
        yl_ref[0, pl.ds(c0, LRU_ROWS), :] = (_get(ufl, n_lat, c0, LRU_ROWS)
                                             + _get(ubl, n_lat, c0, LRU_ROWS)).astype(BF16)


def _lru(x_lat, x_ctx, conv_w, conv_b, wa, ba, wx, bx, lam):
    b, n_lat, w = x_lat.shape
    n_ctx = x_ctx.shape[1]
    consts = (conv_w, conv_b, wa, ba, wx, bx, lam)
    big = lambda: pltpu.VMEM((LRU_HALVES, LRU_SEG * _pitch(n_lat), LANES), F32)
    small = lambda: pltpu.VMEM((LRU_HALVES, LRU_SEG * _pitch(n_ctx), LANES), F32)
    return pl.pallas_call(
        _lru_kernel,
        grid=(b,),
        in_specs=[pl.BlockSpec((1, n_lat, w), lambda bb: (bb, 0, 0)),
                  pl.BlockSpec((1, n_ctx, w), lambda bb: (bb, 0, 0))]
        + [_const_spec(c.shape) for c in consts],
        out_specs=[pl.BlockSpec((1, n_lat, w), lambda bb: (bb, 0, 0)),
                   pl.BlockSpec((1, n_ctx, w), lambda bb: (bb, 0, 0))],
        out_shape=[jax.ShapeDtypeStruct((b, n_lat, w), BF16),
                   jax.ShapeDtypeStruct((b, n_ctx, w), BF16)],
        scratch_shapes=[pltpu.VMEM((n_lat + 2 * LRU_PAD, w), F32),
                        big(), big(), big(), big(), small(), small(), small(), small()],
        compiler_params=_params(1),
        name="rglru",
    )(x_lat, x_ctx, *consts)


MERGE_SUB_ROWS = 256


def _merge_kernel(x_ref, mod_ref, ya_ref, yb_ref, yc_ref, yd_ref, wz_ref, wm_ref, wb_ref, wo_ref,
                  lng_ref, lnb_ref, o_ref):
    shift = mod_ref[0, 0:1, :]
    scale = mod_ref[0, 1:2, :]
    gate = mod_ref[0, 2:3, :]
    tile = x_ref.shape[1]
    sub = min(tile, MERGE_SUB_ROWS)
    for r0 in range(0, tile, sub):
        rows = pl.ds(r0, sub)
        x = x_ref[0, rows, :]
        xm = (_layer_norm(x) * (1.0 + scale) + shift).astype(BF16)
        z = _dot(xm, wz_ref[...])
        acc = None
        for i, y_ref in enumerate((ya_ref, yb_ref, yc_ref, yd_ref)):
            zi = z[:, BRANCH_W * i:BRANCH_W * (i + 1)]
            g = (y_ref[0, rows, :].astype(F32) * (zi * _sigmoid(zi))).astype(BF16)
            t = _dot(g, wb_ref[i])
            mi = _dot(xm, wm_ref[:, D_MODEL * i:D_MODEL * (i + 1)])
            term = _sigmoid(mi) * t
            acc = term if acc is None else acc + term
        out = _dot(acc.astype(BF16), wo_ref[...])
        o_ref[0, rows, :] = _layer_norm(DEEPNORM_ALPHA * x + gate * out) * lng_ref[...] + lnb_ref[...]


def _merge(x, mod, ys, wz, wm, wb, wo, ln_g, ln_b, tile):
    b, n, _ = x.shape
    consts = (wz, wm, wb, wo, ln_g, ln_b)
    tok = lambda w: pl.BlockSpec((1, tile, w), lambda bb, t: (bb, t, 0))
    return pl.pallas_call(
        _merge_kernel,
        grid=(b, n // tile),
        in_specs=[tok(D_MODEL), pl.BlockSpec((1, 3, D_MODEL), lambda bb, t: (bb, 0, 0))]
        + [tok(BRANCH_W) for _ in ys] + [_const_spec(c.shape, single_buffer=True) for c in consts],
        out_specs=tok(D_MODEL),
        out_shape=jax.ShapeDtypeStruct((b, n, D_MODEL), F32),
        compiler_params=_params(2),
        name="merge",
    )(x, mod, *ys, *consts)


def _rope_tables(n):
    t = jnp.arange(n, dtype=jnp.int32)
    rows = (t // GRID_W).astype(F32)[:, None]
    cols = (t % GRID_W).astype(F32)[:, None]

    def table(half, offsets):
        inv = ROPE_THETA ** (-np.arange(0, 2 * half, 2, dtype=np.float64) / (2 * half))
        f_row, f_col, sign = (np.zeros(LANES, np.float32) for _ in range(3))
        for offset in offsets:
            for g, f in enumerate((f_row, f_col)):
                base = offset + 2 * half * g
                f[base:base + 2 * half] = np.concatenate([inv, inv])
                sign[base:base + 2 * half] = np.concatenate([-np.ones(half), np.ones(half)])
        ang = rows * f_row[None, :] + cols * f_col[None, :]
        return jnp.cos(ang), jnp.sin(ang) * sign[None, :]

    cosd, sind = table(16, (0, HEAD_DIM))
    cosb, sinb = table(8, (MLA_NOPE,))
    cosk, sink = table(8, (0,))
    return cosd, sind, cosb, sinb, cosk, sink


def _identity_tables(n):
    one, zero = jnp.ones((n, LANES), F32), jnp.zeros((n, LANES), F32)
    return one, zero, one, zero, one, zero


def _layer_weights(w_in, mla_q_norm, mla_w_uq, mla_kv_norm, mla_w_ukv, gqa_q_norm, gqa_k_norm, w_branch, w_out):
    w_in = w_in.astype(BF16)
    wi = w_in[:, :MIX_COLS]
    qa, ka, va = wi[:, 0:256], wi[:, 256:512], wi[:, 512:768]
    cq, ckv, kr = wi[:, 768:1024], wi[:, 1024:1152], wi[:, 1152:1184]
    lru, qd, kd, vd = wi[:, 1184:1440], wi[:, 1440:1696], wi[:, 1696:1824], wi[:, 1824:1952]
    head_order = jnp.array([0, 2, 1, 3])
    qd = qd.reshape(D_MODEL, 4, HEAD_DIM)[:, head_order].reshape(D_MODEL, 256)
    wmix = jnp.concatenate([qa, ka, va, cq, ckv, lru, qd, kd, vd, kr,
                            jnp.zeros((D_MODEL, MIX_PAD - MIX_COLS), BF16)], -1)
    hq = MLA_NOPE + MLA_ROPE
    wuq = jnp.pad(mla_w_uq.reshape(256, MLA_HEADS, hq), ((0, 0), (0, 0), (0, LANES - hq)))
    wuq = wuq.reshape(256, MLA_HEADS * LANES).astype(BF16)
    ukv = mla_w_ukv.reshape(128, MLA_HEADS, 128)
    wuk = jnp.pad(ukv[:, :, :MLA_NOPE], ((0, 0), (0, 0), (0, LANES - MLA_NOPE)))
    wuk = wuk.reshape(128, MLA_HEADS * LANES).astype(BF16)
    wuv = ukv[:, :, MLA_NOPE:].reshape(128, 256).astype(BF16)
    pk = np.zeros((LANES, MLA_HEADS * LANES), np.float32)
    for h in range(MLA_HEADS):
        pk[np.arange(MLA_ROPE), LANES * h + MLA_NOPE + np.arange(MLA_ROPE)] = 1.0
    grp = np.kron(np.eye(4, dtype=np.float32), np.ones((HEAD_DIM, HEAD_DIM), np.float32))
    inproj_w = (wmix, wuq, wuk, wuv, jnp.asarray(pk, BF16), jnp.asarray(grp, BF16),
                mla_q_norm.reshape(1, 256), mla_kv_norm.reshape(1, 128),
                jnp.tile(gqa_q_norm, 4).reshape(1, 256), jnp.tile(gqa_k_norm, 2).reshape(1, 128))
    wz = w_in[:, MIX_COLS:MIX_COLS + SILU_COLS]
    wz_d = wz[:, 3 * BRANCH_W:].reshape(D_MODEL, 4, HEAD_DIM)[:, head_order].reshape(D_MODEL, BRANCH_W)
    wz = jnp.concatenate([wz[:, :3 * BRANCH_W], wz_d], -1)
    wm = w_in[:, MIX_COLS + SILU_COLS:]
    wb_d = w_branch[3].reshape(4, HEAD_DIM, D_MODEL)[head_order].reshape(BRANCH_W, D_MODEL)
    wb = jnp.concatenate([w_branch[:3], wb_d[None]], 0).astype(BF16)
    return inproj_w, (wz, wm, wb, w_out.astype(BF16))


def _block_diag(w):
    eye = jnp.eye(4, dtype=w.dtype)
    return jnp.einsum("dkce,kj->dkcje", w, eye).reshape(2, 256, 256)


def kernel(x, c, ctx, c_ctx, w_mod, b_mod, w_in, na_rel_bias, mla_q_norm, mla_w_uq, mla_kv_norm, mla_w_ukv,
           lru_conv_w, lru_conv_b, lru_w_a, lru_b_a, lru_w_x, lru_b_x, lru_lambda, gqa_q_norm, gqa_k_norm,
           w_branch, w_out, ln_g, ln_b):
    b, n, _ = x.shape
    n_ctx = ctx.shape[1]
    rows = 8 * ((b + 1 + 7) // 8)
    c_all = jnp.zeros((rows, D_MODEL), F32).at[:b].set(c).at[b].set(c_ctx)
    mod_all = _modulation(c_all, w_mod, b_mod)
    lat_tables = _rope_tables(n)
    ctx_tables = _identity_tables(n_ctx)

    for l in range(DEPTH):
        need_ctx = l < DEPTH - 1
        mod_lat = mod_all[l, :b].reshape(b, 3, D_MODEL)
        mod_ctx = jnp.broadcast_to(mod_all[l, b].reshape(1, 3, D_MODEL), (b, 3, D_MODEL))
        inproj_w, merge_w = _layer_weights(w_in[l], mla_q_norm[l], mla_w_uq[l], mla_kv_norm[l], mla_w_ukv[l],
                                           gqa_q_norm[l], gqa_k_norm[l], w_branch[l], w_out[l])
        qa, ka, va, qb, kb, vb, xr, qd, kd, vd, vat, vbt, vdt = _inproj(x, mod_lat, inproj_w, lat_tables, 1024)
        qa_c, ka_c, va_c, qb_c, kb_c, vb_c, xr_c, qd_c, kd_c, vd_c, vat_c, vbt_c, vdt_c = _inproj(
            ctx, mod_ctx, inproj_w, ctx_tables, n_ctx)

        ya = _na_attention(qa, ka, vat, ka_c, vat_c, _na_bias(na_rel_bias[l]))
        yb = _attention_t(qb, kb, vbt, kb_c, vbt_c, mode="B", tq=2048, ahead=4)
        yd = _attention_t(qd, kd, vdt, kd_c, vdt_c, mode="D", tq=2048, ahead=5)
        yc, yc_c = _lru(xr, xr_c, lru_conv_w[l], lru_conv_b[l].reshape(1, -1),
                        _block_diag(lru_w_a[l]).astype(BF16), lru_b_a[l],
                        _block_diag(lru_w_x[l]).astype(BF16), lru_b_x[l], lru_lambda[l])
        tail = (*merge_w, ln_g[l].reshape(1, -1), ln_b[l].reshape(1, -1))
        x_new = _merge(x, mod_lat, (ya, yb, yc, yd), *tail, 1024)
        if need_ctx:
            ya_c = _attention_ctx(qa_c, ka_c, va_c, mode="A")
            yb_c = _attention_ctx(qb_c, kb_c, vb_c, mode="B")
            yd_c = _attention_ctx(qd_c, kd_c, vd_c, mode="D")
            ctx = _merge(ctx, mod_ctx, (ya_c, yb_c, yc_c, yd_c), *tail, n_ctx)
        x = x_new
    return x
```
